```python
import math
import jax, jax.numpy as jnp
from jax import lax
import numpy as np

D_MODEL = 1024
BATCH = 16
SEQ = 256
DEPTH = 4
DEC_BATCH = 2
DEC_SEQ = 4096
PAST_LEN = 256

GRID_W = 64
MIX_W = D_MODEL // 2
N_BRANCH = 4
H_A = 4
DK_A = MIX_W // H_A
DV_A = MIX_W // H_A
W_B = MIX_W
LRU_BLOCKS = 8
LRU_BW = W_B // LRU_BLOCKS
LRU_C = 8.0
H_C = 4
DK_C = MIX_W // H_C
DV_C = MIX_W // H_C
H_D = 4
DK_D = MIX_W // H_D
DV_D = MIX_W // H_D
W_D = MIX_W
CONV_K = 4
CONV_LEFT = 2
CHUNK = 64
CHUNK_D = 16
ROPE_BASE = 10000.0
EPS = 1e-6
IN_SIZES = (MIX_W, MIX_W, MIX_W, MIX_W,
            MIX_W, MIX_W,
            MIX_W, MIX_W, MIX_W, MIX_W, 2 * H_C, 2 * H_C,
            MIX_W, 2 * MIX_W, MIX_W, MIX_W,
            N_BRANCH * D_MODEL)
IN_COLS = sum(IN_SIZES)

kernel_name = 'bidir_hybrid_retention_rglru_gdn_hgrn2_prefix_ctx'


def rms_norm(x, g):
    xf = x.astype(jnp.float32)
    y = xf * lax.rsqrt(jnp.mean(xf * xf, axis=-1, keepdims=True) + EPS)
    return (y * g.astype(jnp.float32)).astype(x.dtype)


def head_norm(x):
    return x * lax.rsqrt(jnp.mean(x * x, axis=-1, keepdims=True) + EPS)


def l2_norm(x):
    return x * lax.rsqrt(jnp.sum(x * x, axis=-1, keepdims=True) + EPS)


def flip(t):
    return jnp.flip(t, axis=1)


def conv_centred(x, w):
    n = x.shape[1]
    xp = jnp.pad(x, ((0, 0), (CONV_LEFT, CONV_K - 1 - CONV_LEFT), (0, 0)))
    return sum(xp[:, j:j + n] * w[j] for j in range(CONV_K))


def grid_rope(rows):
    n_freq = DK_A // 4
    inv = ROPE_BASE ** (-jnp.arange(n_freq, dtype=jnp.float32) / n_freq)
    r = jnp.repeat(jnp.arange(rows, dtype=jnp.float32), GRID_W)
    col = jnp.tile(jnp.arange(GRID_W, dtype=jnp.float32), rows)
    ang = jnp.concatenate([r[:, None] * inv, col[:, None] * inv], axis=-1)
    return jnp.cos(ang), jnp.sin(ang)


def apply_rope(x, cos, sin):
    half = x.shape[-1] // 2
    x1, x2 = x[..., :half], x[..., half:]
    cos, sin = cos[None, :, None, :], sin[None, :, None, :]
    return jnp.concatenate([x1 * cos - x2 * sin, x2 * cos + x1 * sin], axis=-1)


def chunk_state_scan(s0, chunk_decay, chunk_kv):
    def step(s, inp):
        dec, kv = inp
        return s * dec[..., None] + kv, s
    s_fin, s_prev = lax.scan(step, s0, (jnp.moveaxis(chunk_decay, 1, 0), jnp.moveaxis(chunk_kv, 1, 0)))
    return s_fin, jnp.moveaxis(s_prev, 0, 1)


def retention_dir(q, k, v, log_gamma, s0):
    b, n_tok, h, dk = q.shape
    dv = v.shape[-1]
    nc = n_tok // CHUNK
    q = q.reshape(b, nc, CHUNK, h, dk)
    k = k.reshape(b, nc, CHUNK, h, dk)
    v = v.reshape(b, nc, CHUNK, h, dv)
    pos = jnp.arange(CHUNK, dtype=jnp.float32)
    rel = pos[:, None] - pos[None, :]
    causal = rel >= 0
    dmat = jnp.where(causal[None], jnp.exp(jnp.where(causal, rel, 0.0)[None] * log_gamma[:, None, None]), 0.0)
    scores = jnp.einsum('bnihd,bnjhd->bnhij', q, k) * dmat
    o_intra = jnp.einsum('bnhij,bnjhe->bnihe', scores, v)
    q_dec = jnp.exp((pos + 1.0)[:, None] * log_gamma)
    k_dec = jnp.exp((CHUNK - 1.0 - pos)[:, None] * log_gamma)
    kv = jnp.einsum('bnjhd,jh,bnjhe->bnhde', k, k_dec, v)
    chunk_dec = jnp.broadcast_to(jnp.exp(CHUNK * log_gamma)[:, None], (b, nc, h, dk))
    s_fin, s_prev = chunk_state_scan(s0, chunk_dec, kv)
    o_inter = jnp.einsum('bnihd,ih,bnhde->bnihe', q, q_dec, s_prev)
    return (o_intra + o_inter).reshape(b, n_tok, h, dv), s_fin


def rglru_dir(x, gate_w, gate_b, lam, h0):
    b, n_tok, w = x.shape
    xb = x.reshape(b, n_tok, LRU_BLOCKS, LRU_BW)
    gates = jnp.einsum('bnkc,gkcd->gbnkd', xb, gate_w).reshape(2, b, n_tok, w) + gate_b[:, None, None, :]
    r = jax.nn.sigmoid(gates[0])
    i = jax.nn.sigmoid(gates[1])
    log_a = -LRU_C * jax.nn.softplus(-lam) * r
    a = jnp.exp(log_a)
    u = jnp.sqrt(jnp.maximum(-jnp.expm1(2.0 * log_a), 1e-12)) * (i * x)
    u = u.at[:, 0].add(a[:, 0] * h0)

    def combine(lhs, rhs):
        return (lhs[0] * rhs[0], rhs[0] * lhs[1] + rhs[1])

    _, hs = lax.associative_scan(combine, (a, u), axis=1)
    return hs, hs[:, -1]


def gated_delta_dir(q, k, v, g, beta, s0):
    b, n_tok, h, dk = q.shape
    dv = v.shape[-1]
    nc = n_tok // CHUNK
    q = (q * dk ** -0.5).reshape(b, nc, CHUNK, h, dk)
    k = k.reshape(b, nc, CHUNK, h, dk)
    v = v.reshape(b, nc, CHUNK, h, dv)
    g = g.reshape(b, nc, CHUNK, h)
    beta = beta.reshape(b, nc, CHUNK, h)
    gc = jnp.cumsum(g, axis=2)
    rel = gc[:, :, :, None, :] - gc[:, :, None, :, :]
    tri = jnp.tril(jnp.ones((CHUNK, CHUNK), bool))[:, :, None]
    strict = jnp.tril(jnp.ones((CHUNK, CHUNK), bool), -1)[:, :, None]
    gam = jnp.where(tri, jnp.exp(jnp.where(tri, rel, 0.0)), 0.0)
    kk = jnp.einsum('bnihd,bnjhd->bnijh', k, k)
    a_mat = jnp.where(strict, beta[:, :, :, None, :] * kk * gam, 0.0)
    lhs = jnp.moveaxis(a_mat, -1, 2) + jnp.eye(CHUNK, dtype=a_mat.dtype)
    rhs_v = jnp.moveaxis(v * beta[..., None], 3, 2)
    rhs_k = jnp.moveaxis(k * (beta * jnp.exp(gc))[..., None], 3, 2)
    u = lax.linalg.triangular_solve(lhs, rhs_v, left_side=True, lower=True, unit_diagonal=True)
    w = lax.linalg.triangular_solve(lhs, rhs_k, left_side=True, lower=True, unit_diagonal=True)
    attn = jnp.moveaxis(jnp.einsum('bnihd,bnjhd->bnijh', q, k) * gam, -1, 2)
    q_dec = jnp.moveaxis(q * jnp.exp(gc)[..., None], 3, 2)
    k_dec = jnp.moveaxis(k * jnp.exp(gc[:, :, -1:] - gc)[..., None], 3, 2)
    g_last = jnp.exp(gc[:, :, -1])

    def step(s, inp):
        u_c, w_c, attn_c, qd_c, kd_c, gl_c = inp
        v_new = u_c - jnp.einsum('bhcd,bhde->bhce', w_c, s)
        o_c = jnp.einsum('bhcd,bhde->bhce', qd_c, s) + jnp.einsum('bhij,bhje->bhie', attn_c, v_new)
        s = s * gl_c[..., None, None] + jnp.einsum('bhcd,bhce->bhde', kd_c, v_new)
        return s, o_c

    xs = (jnp.moveaxis(u, 1, 0), jnp.moveaxis(w, 1, 0), jnp.moveaxis(attn, 1, 0),
          jnp.moveaxis(q_dec, 1, 0), jnp.moveaxis(k_dec, 1, 0), jnp.moveaxis(g_last, 1, 0))
    s_fin, o = lax.scan(step, s0, xs)
    o = jnp.moveaxis(jnp.moveaxis(o, 0, 1), 2, 3).reshape(b, n_tok, h, dv)
    return o, s_fin


def hgrn2_dir(q, k, v, log_f, s0):
    b, n_tok, h, dk = q.shape
    dv = v.shape[-1]
    nc = n_tok // CHUNK_D
    q = (q * dk ** -0.5).reshape(b, nc, CHUNK_D, h, dk)
    k = k.reshape(b, nc, CHUNK_D, h, dk)
    v = v.reshape(b, nc, CHUNK_D, h, dv)
    bc = jnp.cumsum(log_f.reshape(b, nc, CHUNK_D, h, dk), axis=2)
    tri = jnp.tril(jnp.ones((CHUNK_D, CHUNK_D), bool))[:, :, None, None]
    rel = bc[:, :, :, None] - bc[:, :, None]
    dmat = jnp.where(tri, jnp.exp(jnp.where(tri, rel, 0.0)), 0.0)
    scores = jnp.einsum('bnihd,bnjhd,bnijhd->bnhij', q, k, dmat)
    o_intra = jnp.einsum('bnhij,bnjhe->bnihe', scores, v)
    kv = jnp.einsum('bnjhd,bnjhe->bnhde', k * jnp.exp(bc[:, :, -1:] - bc), v)
    s_fin, s_prev = chunk_state_scan(s0, jnp.exp(bc[:, :, -1]), kv)
    o_inter = jnp.einsum('bnihd,bnhde->bnihe', q * jnp.exp(bc), s_prev)
    return (o_intra + o_inter).reshape(b, n_tok, h, dv), s_fin


def trunk_layer(x, mod_vec, rope, init, l, p):
    f32 = jnp.float32
    bsz, n_tok, _ = x.shape
    shift, scale, gate = jnp.split(jax.nn.silu(mod_vec) @ p['w_mod'][l] + p['b_mod'][l], 3, axis=-1)
    h = rms_norm(x, p['norm_g'][l]) * (1 + scale) + shift
    proj = h @ p['w_in'][l]
    offs = np.cumsum(IN_SIZES)[:-1].tolist()
    (aq, ak, av, az, bx, bz, cq, ck, cv, cz, ca, cb, dq, df, di, dz, mg) = jnp.split(proj, offs, axis=-1)

    def heads(t, nh):
        return t.astype(f32).reshape(bsz, n_tok, nh, -1)

    ret0, lru0, gdn0, hgrn0 = (s.astype(f32) for s in init)

    q, k, v = heads(aq, H_A), heads(ak, H_A), heads(av, H_A)
    if rope is not None:
        q, k = apply_rope(q, *rope), apply_rope(k, *rope)
    k = k * DK_A ** -0.5
    log_gamma = jnp.log1p(-jnp.exp(p['ret_decay'][l].astype(f32)))
    o_f, sr_f = retention_dir(q, k, v, log_gamma[0], ret0[:, 0])
    o_b, sr_b = retention_dir(flip(q), flip(k), flip(v), log_gamma[1], ret0[:, 1])
    out_a = head_norm(o_f + flip(o_b)).reshape(bsz, n_tok, MIX_W) * jax.nn.silu(az.astype(f32))

    xb = conv_centred(bx.astype(f32), p['lru_conv_w'][l].astype(f32)) + p['lru_conv_b'][l].astype(f32)
    gw = p['lru_gate_w'][l].astype(f32)
    gb = p['lru_gate_b'][l].astype(f32)
    lam = p['lru_lambda'][l].astype(f32)
    h_f, hl_f = rglru_dir(xb, gw[0], gb[0], lam[0], lru0[:, 0])
    h_b, hl_b = rglru_dir(flip(xb), gw[1], gb[1], lam[1], lru0[:, 1])
    out_b = (h_f + flip(h_b)) * jax.nn.silu(bz.astype(f32))

    qkv = jax.nn.silu(conv_centred(jnp.concatenate([cq, ck, cv], axis=-1).astype(f32),
                                   p['gdn_conv_w'][l].astype(f32)))
    q, k, v = jnp.split(qkv, 3, axis=-1)
    q, k, v = l2_norm(heads(q, H_C)), l2_norm(heads(k, H_C)), heads(v, H_C)
    a_in = ca.astype(f32).reshape(bsz, n_tok, 2, H_C)
    beta = jax.nn.sigmoid(cb.astype(f32).reshape(bsz, n_tok, 2, H_C))
    g = -jnp.exp(p['gdn_a_log'][l].astype(f32)) * jax.nn.softplus(a_in + p['gdn_dt_bias'][l].astype(f32))
    o_f, sg_f = gated_delta_dir(q, k, v, g[:, :, 0], beta[:, :, 0], gdn0[:, 0])
    o_b, sg_b = gated_delta_dir(flip(q), flip(k), flip(v), flip(g[:, :, 1]), flip(beta[:, :, 1]), gdn0[:, 1])
    out_c = rms_norm(o_f + flip(o_b), p['gdn_norm_g'][l]).reshape(bsz, n_tok, MIX_W) * jax.nn.silu(cz.astype(f32))

    q = jax.nn.silu(heads(dq, H_D))
    vi = heads(di, H_D)
    zf = df.astype(f32).reshape(bsz, n_tok, 2, H_D, DK_D)
    lb = p['lower_bounds'][l].reshape(2, H_D, DK_D)
    k_in = (1.0 - lb) * jax.nn.sigmoid(-zf)
    f_gate = lb + (1.0 - lb) * jax.nn.sigmoid(zf)
    log_f = jnp.log(jnp.maximum(f_gate, 1e-30))
    o_f, sh_f = hgrn2_dir(q, k_in[:, :, 0], vi, log_f[:, :, 0], hgrn0[:, 0])
    o_b, sh_b = hgrn2_dir(flip(q), flip(k_in[:, :, 1]), flip(vi), flip(log_f[:, :, 1]), hgrn0[:, 1])
    out_d = rms_norm(o_f + flip(o_b), p['hgrn_norm_g'][l]).reshape(bsz, n_tok, MIX_W) * jax.nn.silu(dz.astype(f32))

    br = jnp.stack([out_a, out_b, out_c, out_d], axis=2).astype(x.dtype)
    yb = jnp.einsum('bnkw,kwd->bnkd', br, p['w_branch'][l])
    gates = jax.nn.sigmoid(mg.reshape(bsz, n_tok, N_BRANCH, D_MODEL))
    y = jnp.sum(gates * yb, axis=2) @ p['w_out'][l]
    x = x + gate * y
    states = (jnp.stack([sr_f, sr_b], axis=1), jnp.stack([hl_f, hl_b], axis=1),
              jnp.stack([sg_f, sg_b], axis=1), jnp.stack([sh_f, sh_b], axis=1))
    return x, states


def setup_inputs(seed: int = 0) -> dict:
    key = jax.random.key(seed)
    ks = jax.random.split(key, 32)
    f32 = jnp.float32

    def nrm(k, shape, s):
        return s * jax.random.normal(k, shape, f32)

    a0 = jax.random.uniform(ks[17], (DEPTH, 2, W_B), f32, 0.9, 0.999)
    s_root = a0 ** (1.0 / LRU_C)
    dt = jnp.exp(jax.random.uniform(ks[20], (DEPTH, 2, H_C), f32, math.log(1e-3), math.log(1e-1)))
    ret_base = math.log(2.0) * (-5.0 - jnp.arange(H_A, dtype=f32))
    return {
        'x_prompt': nrm(ks[0], (BATCH, SEQ, D_MODEL), 1.0),
        'x_sample': nrm(ks[1], (DEC_BATCH, DEC_SEQ, D_MODEL), 1.0),
        'state_ret': nrm(ks[2], (DEC_BATCH, DEPTH, 2, H_A, DK_A, DV_A), 0.5),
        'state_lru': nrm(ks[3], (DEC_BATCH, DEPTH, 2, W_B), 0.5),
        'state_gdn': nrm(ks[4], (DEC_BATCH, DEPTH, 2, H_C, DK_C, DV_C), 0.1),
        'state_hgrn': nrm(ks[5], (DEC_BATCH, DEPTH, 2, H_D, DK_D, DV_D), 0.5),
        'c': nrm(ks[6], (DEC_BATCH, D_MODEL), 1.0),
        'c_ctx': nrm(ks[7], (D_MODEL,), 1.0),
        'norm_g': 1.0 + nrm(ks[8], (DEPTH, D_MODEL), 0.02),
        'w_mod': nrm(ks[9], (DEPTH, D_MODEL, 3 * D_MODEL), 0.5 * D_MODEL ** -0.5),
        'b_mod': nrm(ks[10], (DEPTH, 3 * D_MODEL), 0.02),
        'w_in': nrm(ks[11], (DEPTH, D_MODEL, IN_COLS), D_MODEL ** -0.5),
        'ret_decay': ret_base + nrm(ks[12], (DEPTH, 2, H_A), 0.05),
        'lru_conv_w': nrm(ks[13], (DEPTH, CONV_K, W_B), CONV_K ** -0.5),
        'lru_conv_b': nrm(ks[14], (DEPTH, W_B), 0.02),
        'lru_gate_w': nrm(ks[15], (DEPTH, 2, 2, LRU_BLOCKS, LRU_BW, LRU_BW), LRU_BW ** -0.5),
        'lru_gate_b': nrm(ks[16], (DEPTH, 2, 2, W_B), 0.02),
        'lru_lambda': jnp.log(s_root) - jnp.log1p(-s_root),
        'gdn_conv_w': nrm(ks[18], (DEPTH, CONV_K, 3 * MIX_W), CONV_K ** -0.5),
        'gdn_a_log': jnp.log(jax.random.uniform(ks[19], (DEPTH, 2, H_C), f32, 1.0, 16.0)),
        'gdn_dt_bias': dt + jnp.log(-jnp.expm1(-dt)),
        'gdn_norm_g': 1.0 + nrm(ks[21], (DEPTH, DV_C), 0.02),
        'hgrn_lb': nrm(ks[22], (DEPTH, 2, W_D), 0.1),
        'hgrn_norm_g': 1.0 + nrm(ks[23], (DEPTH, DV_D), 0.02),
        'w_branch': nrm(ks[24], (DEPTH, N_BRANCH, MIX_W, D_MODEL), MIX_W ** -0.5),
        'w_out': nrm(ks[25], (DEPTH, D_MODEL, D_MODEL), D_MODEL ** -0.5),
        'final_norm_g': 1.0 + nrm(ks[26], (D_MODEL,), 0.02),
    }


def reference(x_prompt, x_sample, state_ret, state_lru, state_gdn, state_hgrn, c, c_ctx,
              norm_g, w_mod, b_mod, w_in, ret_decay, lru_conv_w, lru_conv_b, lru_gate_w,
              lru_gate_b, lru_lambda, gdn_conv_w, gdn_a_log, gdn_dt_bias, gdn_norm_g,
              hgrn_lb, hgrn_norm_g, w_branch, w_out, final_norm_g):
    f32 = jnp.float32
    lb_sm = jax.nn.softmax(hgrn_lb.astype(f32), axis=0)
    lower_bounds = jnp.cumsum(lb_sm, axis=0) - lb_sm[0]
    p = dict(norm_g=norm_g, w_mod=w_mod, b_mod=b_mod, w_in=w_in, ret_decay=ret_decay,
             lru_conv_w=lru_conv_w, lru_conv_b=lru_conv_b, lru_gate_w=lru_gate_w,
             lru_gate_b=lru_gate_b, lru_lambda=lru_lambda, gdn_conv_w=gdn_conv_w,
             gdn_a_log=gdn_a_log, gdn_dt_bias=gdn_dt_bias, gdn_norm_g=gdn_norm_g,
             lower_bounds=lower_bounds, hgrn_norm_g=hgrn_norm_g, w_branch=w_branch, w_out=w_out)

    bp = x_prompt.shape[0]
    zero_init = (jnp.zeros((bp, 2, H_A, DK_A, DV_A), f32), jnp.zeros((bp, 2, W_B), f32),
                 jnp.zeros((bp, 2, H_C, DK_C, DV_C), f32), jnp.zeros((bp, 2, H_D, DK_D, DV_D), f32))
    mod_ctx = c_ctx[None, None, :]
    xc = x_prompt
    ret_l, lru_l, gdn_l, hgrn_l = [], [], [], []
    for l in range(DEPTH):
        xc, (s_r, s_l, s_g, s_h) = trunk_layer(xc, mod_ctx, None, zero_init, l, p)
        ret_l.append(s_r)
        lru_l.append(s_l)
        gdn_l.append(s_g)
        hgrn_l.append(s_h)
    y_prompt = rms_norm(xc, final_norm_g)
    new_ret = jnp.stack(ret_l, axis=1)
    new_lru = jnp.stack(lru_l, axis=1)
    new_gdn = jnp.stack(gdn_l, axis=1)
    new_hgrn = jnp.stack(hgrn_l, axis=1)

    rows = x_sample.shape[1] // GRID_W
    rope = grid_rope(rows)
    mod_lat = c[:, None, :]
    xs = x_sample
    for l in range(DEPTH):
        init = (state_ret[:, l], state_lru[:, l], state_gdn[:, l], state_hgrn[:, l])
        xs, _ = trunk_layer(xs, mod_lat, rope, init, l, p)
    y_sample = rms_norm(xs, final_norm_g)
    return (y_prompt, y_sample, new_ret, new_lru, new_gdn, new_hgrn)
```

```python
import functools

import jax
import jax.numpy as jnp
from jax import lax
from jax.experimental import pallas as pl
from jax.experimental.pallas import tpu as pltpu

F32 = jnp.float32
BF16 = jnp.bfloat16
HIGHEST = lax.Precision.HIGHEST

D_MODEL = 1024
DEPTH = 4
MIX_W = 512
N_BRANCH = 4
N_HEAD = 4
D_HEAD = 128
GRID_W = 64
LRU_C = 8.0
CONV_K = 4
CONV_LEFT = 2
CHUNK = 64
SUB = 16
ROPE_BASE = 10000.0
EPS = 1e-6
LANES = 128
SUBLANES = 8
VMEM_LIMIT = 56 * 1024 * 1024

CB_AQ, CB_AK, CB_AV, CB_BX, CB_CQ, CB_CK, CB_CV, CB_DQ, CB_DF, CB_DI, CB_CAB = (
    0, 4, 8, 12, 16, 20, 24, 28, 32, 40, 44)
MIX_COLS = 45 * LANES
TN_IN = 1920


def _dot(a, b, precision=None):
    return jnp.dot(a, b, preferred_element_type=F32, precision=precision)


def _dot_nt(a, b, precision=None):
    return lax.dot_general(a, b, (((1,), (1,)), ((), ())),
                           preferred_element_type=F32, precision=precision)


def _dot_tn(a, b, precision=None):
    return lax.dot_general(a, b, (((0,), (0,)), ((), ())),
                           preferred_element_type=F32, precision=precision)


def _order_mask(n, reverse, strict):
    i = lax.broadcasted_iota(jnp.int32, (n, n), 0)
    j = lax.broadcasted_iota(jnp.int32, (n, n), 1)
    if reverse:
        return (j > i) if strict else (j >= i)
    return (j < i) if strict else (j <= i)


def _chunk_start(c, nc, reverse):
    cc = (nc - 1 - c) if reverse else c
    return cc, pl.multiple_of(cc * CHUNK, CHUNK)


def _conv_chunk(x_ref, cc, nc, n_tok, w):
    r0 = pl.multiple_of(cc * CHUNK, CHUNK)
    cur = x_ref[pl.ds(r0, CHUNK), :]
    p0 = pl.multiple_of(jnp.maximum(r0 - SUBLANES, 0), SUBLANES)
    n0 = pl.multiple_of(jnp.minimum(r0 + CHUNK, n_tok - SUBLANES), SUBLANES)
    prev = x_ref[pl.ds(p0, SUBLANES), :] * (cc > 0).astype(F32)
    nxt = x_ref[pl.ds(n0, SUBLANES), :] * (cc < nc - 1).astype(F32)
    xw = jnp.concatenate([prev, cur, nxt], axis=0)
    rows = CHUNK + 2 * SUBLANES
    acc = None
    for j in range(CONV_K):
        shift = (CONV_LEFT - j) % rows
        rolled = xw if shift == 0 else pltpu.roll(xw, shift, 0)
        term = rolled[SUBLANES:SUBLANES + CHUNK, :] * w[j:j + 1, :]
        acc = term if acc is None else acc + term
    return acc


def _silu(x):
    return x * jax.nn.sigmoid(x)


def _rms(x):
    return x * lax.rsqrt(jnp.mean(x * x, axis=-1, keepdims=True) + EPS)


def _mod_kernel(m_ref, w_ref, b_ref, o_ref):
    m = _silu(m_ref[...])
    o_ref[...] = _dot(m.astype(BF16), w_ref[...].astype(BF16)) + b_ref[...]


def _mod_vectors(mvecs, w_mod, b_mod):
    tn = D_MODEL
    return pl.pallas_call(
        _mod_kernel,
        grid=(DEPTH, 3 * D_MODEL // tn),
        in_specs=[
            pl.BlockSpec((SUBLANES, D_MODEL), lambda l, j: (0, 0)),
            pl.BlockSpec((None, D_MODEL, tn), lambda l, j: (l, 0, j)),
            pl.BlockSpec((None, 1, tn), lambda l, j: (l, 0, j)),
        ],
        out_specs=pl.BlockSpec((None, SUBLANES, tn), lambda l, j: (l, 0, j)),
        out_shape=jax.ShapeDtypeStruct((DEPTH, SUBLANES, 3 * D_MODEL), F32),
        compiler_params=pltpu.CompilerParams(vmem_limit_bytes=VMEM_LIMIT),
        name="mod_vectors",
    )(mvecs, w_mod, b_mod.reshape(DEPTH, 1, 3 * D_MODEL))


def _modulated_norm(x, g, mod):
    return _rms(x) * g * (1.0 + mod[1:2, :]) + mod[0:1, :]


def _inproj_kernel(x_ref, mod_ref, g_ref, w_ref, o_ref, h_ref):
    @pl.when(pl.program_id(1) == 0)
    def _():
        h_ref[...] = _modulated_norm(x_ref[...], g_ref[...], mod_ref[...]).astype(BF16)

    o_ref[...] = _dot(h_ref[...], w_ref[...])


def _in_projection(x, mods, mod_row, g, w1, tm):
    t = x.shape[0]
    return pl.pallas_call(
        _inproj_kernel,
        grid=(t // tm, MIX_COLS // TN_IN),
        in_specs=[
            pl.BlockSpec((tm, D_MODEL), lambda i, j: (i, 0)),
            pl.BlockSpec((None, 3, D_MODEL), lambda i, j: (mod_row(i), 0, 0)),
            pl.BlockSpec((1, D_MODEL), lambda i, j: (0, 0)),
            pl.BlockSpec((D_MODEL, TN_IN), lambda i, j: (0, j)),
        ],
        out_specs=pl.BlockSpec((tm, TN_IN), lambda i, j: (i, j)),
        out_shape=jax.ShapeDtypeStruct((t, MIX_COLS), F32),
        scratch_shapes=[pltpu.VMEM((tm, D_MODEL), BF16)],
        compiler_params=pltpu.CompilerParams(
            dimension_semantics=("parallel", "arbitrary"), vmem_limit_bytes=VMEM_LIMIT),
        name="in_projection",
    )(x, mods, g, w1)


def _merge_kernel(x_ref, mod_ref, g_ref, oa_ref, ob_ref, oc_ref, od_ref, wz_ref, wg_ref,
                  wb_ref, wo_ref, *rest, final):
    if final:
        fg_ref, o_ref, y_ref = rest
    else:
        (o_ref,) = rest
    x = x_ref[...]
    mod = mod_ref[...]
    h = _modulated_norm(x, g_ref[...], mod).astype(BF16)
    acc = None
    for k, b_ref in enumerate((oa_ref, ob_ref, oc_ref, od_ref)):
        z = _dot(h, wz_ref[k])
        br = (b_ref[...] * _silu(z)).astype(BF16)
        yb = _dot(br, wb_ref[k])
        term = jax.nn.sigmoid(_dot(h, wg_ref[k])) * yb
        acc = term if acc is None else acc + term
    y = _dot(acc.astype(BF16), wo_ref[...])
    out = x + mod[2:3, :] * y
    o_ref[...] = out
    if final:
        y_ref[...] = _rms(out) * fg_ref[...]


def _merge(x, mods, mod_row, g, outs, wz, wg, wb, wo, final_g, tm):
    t = x.shape[0]
    final = final_g is not None
    const = dict(pipeline_mode=pl.Buffered(1))
    in_specs = [
        pl.BlockSpec((tm, D_MODEL), lambda i: (i, 0)),
        pl.BlockSpec((None, 3, D_MODEL), lambda i: (mod_row(i), 0, 0)),
        pl.BlockSpec((1, D_MODEL), lambda i: (0, 0)),
    ] + [pl.BlockSpec((tm, MIX_W), lambda i: (i, 0)) for _ in range(N_BRANCH)] + [
        pl.BlockSpec((N_BRANCH, D_MODEL, MIX_W), lambda i: (0, 0, 0), **const),
        pl.BlockSpec((N_BRANCH, D_MODEL, D_MODEL), lambda i: (0, 0, 0), **const),
        pl.BlockSpec((N_BRANCH, MIX_W, D_MODEL), lambda i: (0, 0, 0), **const),
        pl.BlockSpec((D_MODEL, D_MODEL), lambda i: (0, 0), **const),
    ]
    args = [x, mods, g, *outs, wz, wg, wb, wo]
    out_specs = [pl.BlockSpec((tm, D_MODEL), lambda i: (i, 0))]
    out_shape = [jax.ShapeDtypeStruct((t, D_MODEL), F32)]
    if final:
        in_specs.append(pl.BlockSpec((1, D_MODEL), lambda i: (0, 0)))
        args.append(final_g)
        out_specs.append(pl.BlockSpec((tm, D_MODEL), lambda i: (i, 0)))
        out_shape.append(jax.ShapeDtypeStruct((t, D_MODEL), F32))
    res = pl.pallas_call(
        functools.partial(_merge_kernel, final=final),
        grid=(t // tm,),
        in_specs=in_specs,
        out_specs=out_specs,
        out_shape=out_shape,
        compiler_params=pltpu.CompilerParams(
            dimension_semantics=("parallel",), vmem_limit_bytes=VMEM_LIMIT),
        name="merge",
    )(*args)
    return res if final else (res[0], None)


def _ret_kernel(lg_ref, q_ref, k_ref, v_ref, *rest, n_tok, rope, has_init):
    rest = list(rest)
    cos_ref, sin_ref = (rest.pop(0), rest.pop(0)) if rope else (None, None)
    s0_ref = rest.pop(0) if has_init else None
    o_ref, sf_ref = rest
    h = pl.program_id(1)
    nc = n_tok // CHUNK
    pos_c = lax.broadcasted_iota(jnp.int32, (CHUNK, 1), 0).astype(F32)
    pos_r = lax.broadcasted_iota(jnp.int32, (1, CHUNK), 1).astype(F32)
    for d in range(2):
        reverse = d == 1
        lg = lg_ref[d, h]
        if reverse:
            p_c, p_r = CHUNK - 1.0 - pos_c, CHUNK - 1.0 - pos_r
        else:
            p_c, p_r = pos_c, pos_r
        rel = p_c - p_r
        causal = rel >= 0
        dmat = jnp.where(causal, jnp.exp(jnp.where(causal, rel, 0.0) * lg), 0.0)
        q_dec = jnp.exp((p_c + 1.0) * lg)
        k_dec = jnp.exp((CHUNK - 1.0 - p_c) * lg)
        c_dec = jnp.exp(jnp.full((1, D_HEAD), CHUNK * lg, F32))

        def body(c, s, reverse=reverse, d=d, dmat=dmat, q_dec=q_dec, k_dec=k_dec, c_dec=c_dec):
            _, r0 = _chunk_start(c, nc, reverse)
            rows = pl.ds(r0, CHUNK)
            q, k, v = q_ref[rows, :], k_ref[rows, :], v_ref[rows, :]
            if rope:
                cs, sn = cos_ref[rows, :], sin_ref[rows, :]
                q = q * cs + pltpu.roll(q, D_HEAD // 2, 1) * sn
                k = k * cs + pltpu.roll(k, D_HEAD // 2, 1) * sn
            k = k * D_HEAD ** -0.5
            scores = _dot_nt(q, k) * dmat
            o = _dot(scores, v) + _dot(q * q_dec, s)
            s = s * c_dec + _dot_tn(k * k_dec, v)
            if d == 0:
                o_ref[rows, :] = o
            else:
                o_ref[rows, :] = o_ref[rows, :] + o
            return s

        s0 = s0_ref[d] if has_init else jnp.zeros((D_HEAD, D_HEAD), F32)
        sf_ref[d] = lax.fori_loop(0, nc, body, s0)

    def norm(c, carry):
        rows = pl.ds(pl.multiple_of(c * CHUNK, CHUNK), CHUNK)
        o_ref[rows, :] = _rms(o_ref[rows, :])
        return carry

    lax.fori_loop(0, nc, norm, 0)


def _state_specs(layer):
    s_in = pl.BlockSpec((None, None, 2, None, D_HEAD, D_HEAD), lambda b, h: (b, layer, 0, h, 0, 0))
    s_out = pl.BlockSpec((None, 2, None, D_HEAD, D_HEAD), lambda b, h: (b, 0, h, 0, 0))
    return s_in, s_out


def _col_spec(n_tok, cb):
    return pl.BlockSpec((n_tok, LANES), lambda b, h: (b, cb + h))


def _head_mixer_call(kernel, name, bsz, n_tok, in_specs, args, scratch=()):
    _, s_out = _state_specs(0)
    return pl.pallas_call(
        kernel,
        grid=(bsz, N_HEAD),
        in_specs=in_specs,
        out_specs=[pl.BlockSpec((n_tok, LANES), lambda b, h: (b, h)), s_out],
        out_shape=[jax.ShapeDtypeStruct((bsz * n_tok, MIX_W), F32),
                   jax.ShapeDtypeStruct((bsz, 2, N_HEAD, D_HEAD, D_HEAD), F32)],
        scratch_shapes=list(scratch),
        compiler_params=pltpu.CompilerParams(
            dimension_semantics=("parallel", "arbitrary"), vmem_limit_bytes=VMEM_LIMIT),
        name=name,
    )(*args)


def _retention(proj, log_gamma, rope, state, layer, bsz, n_tok):
    in_specs = [pl.BlockSpec(memory_space=pltpu.SMEM),
                _col_spec(n_tok, CB_AQ), _col_spec(n_tok, CB_AK), _col_spec(n_tok, CB_AV)]
    args = [log_gamma, proj, proj, proj]
    if rope is not None:
        in_specs += [pl.BlockSpec((n_tok, LANES), lambda b, h: (0, 0))] * 2
        args += list(rope)
    if state is not None:
        in_specs.append(_state_specs(layer)[0])
        args.append(state)
    kernel = functools.partial(_ret_kernel, n_tok=n_tok, rope=rope is not None,
                               has_init=state is not None)
    return _head_mixer_call(kernel, "retention", bsz, n_tok, in_specs, args)


def _tile_scan(a, u, reverse):
    row = lax.broadcasted_iota(jnp.int32, a.shape, 0)
    for s in (1, 2, 4):
        if reverse:
            valid = row < SUBLANES - s
            shift = SUBLANES - s
        else:
            valid = row >= s
            shift = s
        a_sh = jnp.where(valid, pltpu.roll(a, shift, 0), 1.0)
        u_sh = jnp.where(valid, pltpu.roll(u, shift, 0), 0.0)
        u = a * u_sh + u
        a = a * a_sh
    return a, u


def _lru_kernel(x_ref, cw_ref, cb_ref, gw_ref, gb_ref, lam_ref, *rest, n_tok, has_init):
    rest = list(rest)
    h0_ref = rest.pop(0) if has_init else None
    o_ref, hl_ref = rest
    nc = n_tok // CHUNK
    n_tile = CHUNK // SUBLANES
    cw = cw_ref[...]
    cb = cb_ref[...]
    for d in range(2):
        reverse = d == 1
        coef = -LRU_C * jax.nn.softplus(-lam_ref[d])
        w_r, w_i = gw_ref[d, 0], gw_ref[d, 1]
        b_r, b_i = gb_ref[2 * d], gb_ref[2 * d + 1]

        def body(c, carry, reverse=reverse, d=d, coef=coef, w_r=w_r, w_i=w_i, b_r=b_r, b_i=b_i):
            cc, r0 = _chunk_start(c, nc, reverse)
            xb = _conv_chunk(x_ref, cc, nc, n_tok, cw) + cb
            r = jax.nn.sigmoid(_dot(xb, w_r) + b_r)
            i = jax.nn.sigmoid(_dot(xb, w_i) + b_i)
            log_a = coef * r
            a = jnp.exp(log_a)
            one_minus_a2 = -jnp.tanh(log_a) * (a * a + 1.0)
            u = jnp.sqrt(jnp.maximum(one_minus_a2, 1e-12)) * (i * xb)
            tiles = [None] * n_tile
            order = range(n_tile - 1, -1, -1) if reverse else range(n_tile)
            last = 0 if reverse else SUBLANES - 1
            for t in order:
                sl = slice(t * SUBLANES, (t + 1) * SUBLANES)
                a_t, u_t = _tile_scan(a[sl, :], u[sl, :], reverse)
                h_t = a_t * carry + u_t
                carry = h_t[last:last + 1, :]
                tiles[t] = h_t
            hs = jnp.concatenate(tiles, axis=0)
            rows = pl.ds(r0, CHUNK)
            if d == 0:
                o_ref[rows, :] = hs
            else:
                o_ref[rows, :] = o_ref[rows, :] + hs
            return carry

        h0 = h0_ref[d] if has_init else jnp.zeros((1, LANES), F32)
        hl_ref[d] = lax.fori_loop(0, nc, body, h0)


def _rglru(proj, conv_w, conv_b, gate_w, gate_b, lam, state, layer, bsz, n_tok):
    n_pair = MIX_W // LANES
    in_specs = [
        pl.BlockSpec((n_tok, LANES), lambda b, p: (b, CB_BX + p)),
        pl.BlockSpec((CONV_K, LANES), lambda b, p: (0, p)),
        pl.BlockSpec((1, LANES), lambda b, p: (0, p)),
        pl.BlockSpec((2, 2, None, LANES, LANES), lambda b, p: (0, 0, p, 0, 0)),
        pl.BlockSpec((4, 1, LANES), lambda b, p: (0, 0, p)),
        pl.BlockSpec((2, 1, LANES), lambda b, p: (0, 0, p)),
    ]
    args = [proj, conv_w, conv_b, gate_w, gate_b, lam]
    if state is not None:
        in_specs.append(pl.BlockSpec((None, None, 2, 1, LANES), lambda b, p: (b, layer, 0, 0, p)))
        args.append(state)
    return pl.pallas_call(
        functools.partial(_lru_kernel, n_tok=n_tok, has_init=state is not None),
        grid=(bsz, n_pair),
        in_specs=in_specs,
        out_specs=[pl.BlockSpec((n_tok, LANES), lambda b, p: (b, p)),
                   pl.BlockSpec((None, 2, 1, LANES), lambda b, p: (b, 0, 0, p))],
        out_shape=[jax.ShapeDtypeStruct((bsz * n_tok, MIX_W), F32),
                   jax.ShapeDtypeStruct((bsz, 2, 1, MIX_W), F32)],
        compiler_params=pltpu.CompilerParams(
            dimension_semantics=("parallel", "arbitrary"), vmem_limit_bytes=VMEM_LIMIT),
        name="rglru",
    )(*args)


def _unit_lower_inverse(a):
    i = lax.broadcasted_iota(jnp.int32, a.shape, 0)
    j = lax.broadcasted_iota(jnp.int32, a.shape, 1)
    p = -a
    t = jnp.where(i == j, 1.0, 0.0) + p
    steps = CHUNK.bit_length() - 2
    for _ in range(steps):
        p = _dot(p, p, HIGHEST)
        t = t + _dot(t, p, HIGHEST)
    return t


def _gdn_kernel(na_ref, dtb_ref, q_ref, k_ref, v_ref, ab_ref, wq_ref, wk_ref, wv_ref, g_ref,
                *rest, n_tok, has_init):
    rest = list(rest)
    s0_ref = rest.pop(0) if has_init else None
    o_ref, sf_ref, qs_ref, ks_ref, vs_ref = rest
    h = pl.program_id(1)
    nc = n_tok // CHUNK

    def prep(c, carry):
        rows = pl.ds(pl.multiple_of(c * CHUNK, CHUNK), CHUNK)
        q = _silu(_conv_chunk(q_ref, c, nc, n_tok, wq_ref[...]))
        k = _silu(_conv_chunk(k_ref, c, nc, n_tok, wk_ref[...]))
        v = _silu(_conv_chunk(v_ref, c, nc, n_tok, wv_ref[...]))
        q = q * lax.rsqrt(jnp.sum(q * q, axis=-1, keepdims=True) + EPS)
        k = k * lax.rsqrt(jnp.sum(k * k, axis=-1, keepdims=True) + EPS)
        qs_ref[rows, :] = q * D_HEAD ** -0.5
        ks_ref[rows, :] = k
        vs_ref[rows, :] = v
        return carry

    lax.fori_loop(0, nc, prep, 0)

    lane = lax.broadcasted_iota(jnp.int32, (CHUNK, LANES), 1)
    for d in range(2):
        reverse = d == 1
        incl = _order_mask(CHUNK, reverse, strict=False)
        strict = _order_mask(CHUNK, reverse, strict=True)
        incl_f = incl.astype(F32)
        strict_f = strict.astype(F32)
        neg_a = na_ref[d, h]
        dt_bias = dtb_ref[d, h]
        a_lane = d * N_HEAD + h
        b_lane = 2 * N_HEAD + d * N_HEAD + h
        last = 0 if reverse else CHUNK - 1

        def body(c, s, reverse=reverse, d=d, incl=incl, strict=strict, incl_f=incl_f,
                 strict_f=strict_f, neg_a=neg_a, dt_bias=dt_bias, a_lane=a_lane, b_lane=b_lane,
                 last=last):
            _, r0 = _chunk_start(c, nc, reverse)
            rows = pl.ds(r0, CHUNK)
            q, k, v = qs_ref[rows, :], ks_ref[rows, :], vs_ref[rows, :]
            ab = ab_ref[rows, :]
            a_in = jnp.sum(jnp.where(lane == a_lane, ab, 0.0), axis=1, keepdims=True)
            b_in = jnp.sum(jnp.where(lane == b_lane, ab, 0.0), axis=1, keepdims=True)
            g = neg_a * jax.nn.softplus(a_in + dt_bias)
            beta = jax.nn.sigmoid(b_in)
            gc = _dot(incl_f, jnp.broadcast_to(g, (CHUNK, LANES)), HIGHEST)
            rel = _dot(incl_f, g * strict_f, HIGHEST)
            gam = jnp.where(incl, jnp.exp(jnp.where(incl, rel, 0.0)), 0.0)
            e_gc = jnp.exp(gc)
            gl = gc[last:last + 1, :]
            kk = _dot_nt(k, k)
            a_mat = jnp.where(strict, beta * kk * gam, 0.0)
            t_inv = _unit_lower_inverse(a_mat)
            u = _dot(t_inv, v * beta, HIGHEST)
            w = _dot(t_inv, k * (beta * e_gc), HIGHEST)
            attn = _dot_nt(q, k) * gam
            v_new = u - _dot(w, s)
            o = _dot(q * e_gc, s) + _dot(attn, v_new)
            s = s * jnp.exp(gl) + _dot_tn(k * jnp.exp(gl - gc), v_new)
            if d == 0:
                o_ref[rows, :] = o
            else:
                o_ref[rows, :] = o_ref[rows, :] + o
            return s

        s0 = s0_ref[d] if has_init else jnp.zeros((D_HEAD, D_HEAD), F32)
        sf_ref[d] = lax.fori_loop(0, nc, body, s0)

    def norm(c, carry):
        rows = pl.ds(pl.multiple_of(c * CHUNK, CHUNK), CHUNK)
        o_ref[rows, :] = _rms(o_ref[rows, :]) * g_ref[...]
        return carry

    lax.fori_loop(0, nc, norm, 0)


def _gated_delta(proj, neg_a, dt_bias, conv_w, norm_g, state, layer, bsz, n_tok):
    smem = pl.BlockSpec(memory_space=pltpu.SMEM)
    in_specs = [smem, smem,
                _col_spec(n_tok, CB_CQ), _col_spec(n_tok, CB_CK), _col_spec(n_tok, CB_CV),
                pl.BlockSpec((n_tok, LANES), lambda b, h: (b, CB_CAB)),
                pl.BlockSpec((CONV_K, LANES), lambda b, h: (0, h)),
                pl.BlockSpec((CONV_K, LANES), lambda b, h: (0, N_HEAD + h)),
                pl.BlockSpec((CONV_K, LANES), lambda b, h: (0, 2 * N_HEAD + h)),
                pl.BlockSpec((1, D_HEAD), lambda b, h: (0, 0))]
    args = [neg_a, dt_bias, proj, proj, proj, proj, conv_w, conv_w, conv_w, norm_g]
    if state is not None:
        in_specs.append(_state_specs(layer)[0])
        args.append(state)
    kernel = functools.partial(_gdn_kernel, n_tok=n_tok, has_init=state is not None)
    scratch = [pltpu.VMEM((n_tok, D_HEAD), F32)] * 3
    return _head_mixer_call(kernel, "gated_delta", bsz, n_tok, in_specs, args, scratch)


def _block_rows(x, row):
    nb = CHUNK // SUB
    x3 = x.reshape(nb, SUB, LANES)[:, row:row + 1, :]
    return jnp.broadcast_to(x3, (nb, SUB, LANES)).reshape(CHUNK, LANES)


def _hgrn_kernel(q_ref, f0_ref, f1_ref, v_ref, lb_ref, g_ref, *rest, n_tok, has_init):
    rest = list(rest)
    s0_ref = rest.pop(0) if has_init else None
    o_ref, sf_ref = rest
    nc = n_tok // CHUNK
    nb = CHUNK // SUB
    blk_c = lax.broadcasted_iota(jnp.int32, (CHUNK, 1), 0) // SUB
    blk_i = lax.broadcasted_iota(jnp.int32, (CHUNK, CHUNK), 0) // SUB
    blk_j = lax.broadcasted_iota(jnp.int32, (CHUNK, CHUNK), 1) // SUB
    for d in range(2):
        reverse = d == 1
        f_ref = f1_ref if reverse else f0_ref
        lb = lb_ref[d:d + 1, :]
        incl = _order_mask(CHUNK, reverse, strict=False)
        incl_f = incl.astype(F32)
        diag = incl & (blk_i == blk_j)
        first = SUB - 1 if reverse else 0
        mid = SUB // 2 if reverse else SUB // 2 - 1
        last = 0 if reverse else CHUNK - 1

        def body(c, st, reverse=reverse, d=d, f_ref=f_ref, lb=lb, incl_f=incl_f, diag=diag,
                 first=first, mid=mid, last=last):
            _, r0 = _chunk_start(c, nc, reverse)
            rows = pl.ds(r0, CHUNK)
            zf = f_ref[rows, :]
            q = _silu(q_ref[rows, :]) * D_HEAD ** -0.5
            v = v_ref[rows, :]
            k = (1.0 - lb) * jax.nn.sigmoid(-zf)
            f_gate = lb + (1.0 - lb) * jax.nn.sigmoid(zf)
            log_f = jnp.log(jnp.maximum(f_gate, 1e-30))
            bc = _dot(incl_f, log_f, HIGHEST)
            ex = bc - log_f
            m_blk = _block_rows(bc, mid)
            r_blk = _block_rows(ex, first)
            a_mat = jnp.where(diag, _dot_nt(q * jnp.exp(bc - m_blk), k * jnp.exp(m_blk - bc)), 0.0)
            q_c = q * jnp.exp(bc - r_blk)
            cross = []
            for blk in range(nb):
                earlier = (blk_c > blk) if reverse else (blk_c < blk)
                if (blk == nb - 1) if reverse else (blk == 0):
                    cross.append(jnp.zeros((SUB, CHUNK), F32))
                    continue
                r_i = ex[blk * SUB + first:blk * SUB + first + 1, :]
                k_t = jnp.where(earlier, k * jnp.exp(jnp.where(earlier, r_i - bc, 0.0)), 0.0)
                cross.append(_dot_nt(q_c[blk * SUB:(blk + 1) * SUB, :], k_t))
            a_mat = a_mat + jnp.concatenate(cross, axis=0)
            b_last = bc[last:last + 1, :]
            o = _dot(a_mat, v) + _dot_nt(q * jnp.exp(bc), st)
            st = st * jnp.exp(b_last) + _dot_tn(v, k * jnp.exp(b_last - bc))
            if d == 0:
                o_ref[rows, :] = o
            else:
                o_ref[rows, :] = o_ref[rows, :] + o
            return st

        st0 = s0_ref[d].T if has_init else jnp.zeros((D_HEAD, D_HEAD), F32)
        sf_ref[d] = lax.fori_loop(0, nc, body, st0).T

    def norm(c, carry):
        rows = pl.ds(pl.multiple_of(c * CHUNK, CHUNK), CHUNK)
        o_ref[rows, :] = _rms(o_ref[rows, :]) * g_ref[...]
        return carry

    lax.fori_loop(0, nc, norm, 0)


def _hgrn2(proj, lower_bound, norm_g, state, layer, bsz, n_tok):
    in_specs = [_col_spec(n_tok, CB_DQ), _col_spec(n_tok, CB_DF), _col_spec(n_tok, CB_DF + N_HEAD),
                _col_spec(n_tok, CB_DI),
                pl.BlockSpec((2, D_HEAD), lambda b, h: (0, h)),
                pl.BlockSpec((1, D_HEAD), lambda b, h: (0, 0))]
    args = [proj, proj, proj, proj, lower_bound, norm_g]
    if state is not None:
        in_specs.append(_state_specs(layer)[0])
        args.append(state)
    kernel = functools.partial(_hgrn_kernel, n_tok=n_tok, has_init=state is not None)
    return _head_mixer_call(kernel, "hgrn2", bsz, n_tok, in_specs, args)


def _rope_tables(n_tok):
    n_freq = D_HEAD // 4
    inv = ROPE_BASE ** (-jnp.arange(n_freq, dtype=F32) / n_freq)
    rows = n_tok // GRID_W
    r = jnp.repeat(jnp.arange(rows, dtype=F32), GRID_W)
    col = jnp.tile(jnp.arange(GRID_W, dtype=F32), rows)
    ang = jnp.concatenate([r[:, None] * inv, col[:, None] * inv], axis=-1)
    cos, sin = jnp.cos(ang), jnp.sin(ang)
    return jnp.concatenate([cos, cos], axis=-1), jnp.concatenate([-sin, sin], axis=-1)


def _split_w_in(w):
    m = MIX_W
    a = w[:, 0:4 * m]
    b = w[:, 4 * m:6 * m]
    c = w[:, 6 * m:10 * m + 16]
    dd = w[:, 10 * m + 16:15 * m + 16]
    mg = w[:, 15 * m + 16:]
    cab = jnp.pad(c[:, 4 * m:], ((0, 0), (0, LANES - 16)))
    w1 = jnp.concatenate([a[:, :3 * m], b[:, :m], c[:, :3 * m], dd[:, :4 * m], cab], axis=1)
    wz = jnp.stack([a[:, 3 * m:], b[:, m:], c[:, 3 * m:4 * m], dd[:, 4 * m:]], axis=0)
    wg = jnp.moveaxis(mg.reshape(D_MODEL, N_BRANCH, D_MODEL), 1, 0)
    return w1.astype(BF16), wz.astype(BF16), wg.astype(BF16)


def _pair_block_diag(gw):
    z = jnp.zeros_like(gw[:, :, 0::2])
    top = jnp.concatenate([gw[:, :, 0::2], z], axis=-1)
    bot = jnp.concatenate([z, gw[:, :, 1::2]], axis=-1)
    return jnp.concatenate([top, bot], axis=-2)


def kernel(x_prompt, x_sample, state_ret, state_lru, state_gdn, state_hgrn, c, c_ctx, norm_g,
           w_mod, b_mod, w_in, ret_decay, lru_conv_w, lru_conv_b, lru_gate_w, lru_gate_b,
           lru_lambda, gdn_conv_w, gdn_a_log, gdn_dt_bias, gdn_norm_g, hgrn_lb, hgrn_norm_g,
           w_branch, w_out, final_norm_g):
    bp, n_ctx, _ = x_prompt.shape
    bs, n_lat, _ = x_sample.shape

    lb_sm = jax.nn.softmax(hgrn_lb.astype(F32), axis=0)
    lower_bounds = jnp.cumsum(lb_sm, axis=0) - lb_sm[0]
    log_gamma = jnp.log1p(-jnp.exp(ret_decay.astype(F32)))
    gdn_neg_a = -jnp.exp(gdn_a_log.astype(F32))
    rope = _rope_tables(n_lat)
    state_lru5 = state_lru.reshape(bs, DEPTH, 2, 1, MIX_W)

    mvecs = jnp.concatenate([c_ctx[None, :], c, jnp.zeros((SUBLANES - 1 - bs, D_MODEL), F32)], axis=0)
    mods = _mod_vectors(mvecs, w_mod, b_mod).reshape(DEPTH, SUBLANES, 3, D_MODEL)

    tm = 512
    groups = [
        dict(x=x_prompt.reshape(bp * n_ctx, D_MODEL), bsz=bp, n_tok=n_ctx, rope=None, cached=False,
             mod_row=lambda i: 0),
        dict(x=x_sample.reshape(bs * n_lat, D_MODEL), bsz=bs, n_tok=n_lat, rope=rope, cached=True,
             mod_row=lambda i: 1 + (i * tm) // n_lat),
    ]
    new_states = [[], [], [], []]
    finals = []
    for grp in groups:
        x, bsz, n_tok = grp["x"], grp["bsz"], grp["n_tok"]
        cached = grp["cached"]
        for l in range(DEPTH):
            w1, wz, wg = _split_w_in(w_in[l])
            g = norm_g[l][None, :]
            proj = _in_projection(x, mods[l], grp["mod_row"], g, w1, tm)
            o_a, s_a = _retention(proj, log_gamma[l], grp["rope"], state_ret if cached else None,
                                  l, bsz, n_tok)
            gate_b = lru_gate_b[l].reshape(4, 1, MIX_W)
            o_b, s_b = _rglru(proj, lru_conv_w[l], lru_conv_b[l][None, :], _pair_block_diag(lru_gate_w[l]),
                              gate_b, lru_lambda[l][:, None, :], state_lru5 if cached else None,
                              l, bsz, n_tok)
            o_c, s_c = _gated_delta(proj, gdn_neg_a[l], gdn_dt_bias[l].astype(F32), gdn_conv_w[l],
                                    gdn_norm_g[l][None, :], state_gdn if cached else None, l, bsz, n_tok)
            o_d, s_d = _hgrn2(proj, lower_bounds[l], hgrn_norm_g[l][None, :],
                              state_hgrn if cached else None, l, bsz, n_tok)
            final_g = final_norm_g[None, :] if l == DEPTH - 1 else None
            x, y = _merge(x, mods[l], grp["mod_row"], g, (o_a, o_b, o_c, o_d), wz, wg,
                          w_branch[l].astype(BF16), w_out[l].astype(BF16), final_g, tm)
            if not cached:
                for acc, s in zip(new_states, (s_a, s_b, s_c, s_d)):
                    acc.append(s)
        finals.append(y)

    y_prompt = finals[0].reshape(bp, n_ctx, D_MODEL)
    y_sample = finals[1].reshape(bs, n_lat, D_MODEL)
    new_ret = jnp.stack(new_states[0], axis=1)
    new_lru = jnp.stack(new_states[1], axis=1).reshape(bp, DEPTH, 2, MIX_W)
    new_gdn = jnp.stack(new_states[2], axis=1)
    new_hgrn = jnp.stack(new_states[3], axis=1)
    return (y_prompt, y_sample, new_ret, new_lru, new_gdn, new_hgrn)
```

```python
import functools

import jax
import jax.numpy as jnp
from jax import lax
from jax.experimental import pallas as pl
from jax.experimental.pallas import tpu as pltpu

F32 = jnp.float32
BF16 = jnp.bfloat16

D_MODEL = 1024
DEPTH = 4
MIX_W = 512
N_BRANCH = 4
N_HEAD = 4
D_HEAD = 128
GRID_W = 64
LRU_C = 8.0
CONV_K = 4
CONV_LEFT = 2
CHUNK = 64
RET_CHUNK = 256
SUB = 16
GROUP = 4
GROUP_D = 4
ROPE_BASE = 10000.0
EPS = 1e-6
LANES = 128
SUBLANES = 8
VMEM_LIMIT = 56 * 1024 * 1024

CB_AQ, CB_AK, CB_AV, CB_BX, CB_CQ, CB_CK, CB_CV, CB_DQ, CB_DF, CB_DI, CB_CAB = (
    0, 4, 8, 12, 16, 20, 24, 28, 32, 40, 44)
MIX_COLS = 45 * LANES
TN_IN = 1920


def _dot(a, b):
    return jnp.dot(a, b, preferred_element_type=F32)


def _dot_nt(a, b):
    return lax.dot_general(a, b, (((1,), (1,)), ((), ())), preferred_element_type=F32)


def _dot_tn(a, b):
    return lax.dot_general(a, b, (((0,), (0,)), ((), ())), preferred_element_type=F32)


def _split2(x):
    hi = x.astype(BF16)
    lo = (x - hi.astype(F32)).astype(BF16)
    return hi, lo


def _dot_split(a, b):
    return _dot(a[0], b[0]) + (_dot(a[0], b[1]) + _dot(a[1], b[0]))


def _dot_mask(mask_bf16, x):
    hi = x.astype(BF16)
    r1 = x - hi.astype(F32)
    mid = r1.astype(BF16)
    lo = (r1 - mid.astype(F32)).astype(BF16)
    return _dot(mask_bf16, hi) + (_dot(mask_bf16, mid) + _dot(mask_bf16, lo))


def _order_mask(n, reverse, strict):
    i = lax.broadcasted_iota(jnp.int32, (n, n), 0)
    j = lax.broadcasted_iota(jnp.int32, (n, n), 1)
    if reverse:
        return (j > i) if strict else (j >= i)
    return (j < i) if strict else (j <= i)


def _rows(cc, size=CHUNK):
    return pl.ds(pl.multiple_of(cc * size, size), size)


def _conv_chunk(x_ref, cc, nc, n_tok, w):
    r0 = pl.multiple_of(cc * CHUNK, CHUNK)
    cur = x_ref[pl.ds(r0, CHUNK), :]
    p0 = pl.multiple_of(jnp.maximum(r0 - SUBLANES, 0), SUBLANES)
    n0 = pl.multiple_of(jnp.minimum(r0 + CHUNK, n_tok - SUBLANES), SUBLANES)
    prev = x_ref[pl.ds(p0, SUBLANES), :] * (cc > 0).astype(F32)
    nxt = x_ref[pl.ds(n0, SUBLANES), :] * (cc < nc - 1).astype(F32)
    xw = jnp.concatenate([prev, cur, nxt], axis=0)
    rows = CHUNK + 2 * SUBLANES
    acc = None
    for j in range(CONV_K):
        shift = (CONV_LEFT - j) % rows
        rolled = xw if shift == 0 else pltpu.roll(xw, shift, 0)
        term = rolled[SUBLANES:SUBLANES + CHUNK, :] * w[j:j + 1, :]
        acc = term if acc is None else acc + term
    return acc


def _silu(x):
    return x * jax.nn.sigmoid(x)


def _rms(x):
    return x * lax.rsqrt(jnp.mean(x * x, axis=-1, keepdims=True) + EPS)


def _mod_kernel(m_ref, w_ref, b_ref, o_ref):
    m = _silu(m_ref[...])
    o_ref[...] = _dot(m.astype(BF16), w_ref[...].astype(BF16)) + b_ref[...]


def _mod_vectors(mvecs, w_mod, b_mod):
    tn = D_MODEL
    return pl.pallas_call(
        _mod_kernel,
        grid=(DEPTH, 3 * D_MODEL // tn),
        in_specs=[
            pl.BlockSpec((SUBLANES, D_MODEL), lambda l, j: (0, 0)),
            pl.BlockSpec((None, D_MODEL, tn), lambda l, j: (l, 0, j)),
            pl.BlockSpec((None, 1, tn), lambda l, j: (l, 0, j)),
        ],
        out_specs=pl.BlockSpec((None, SUBLANES, tn), lambda l, j: (l, 0, j)),
        out_shape=jax.ShapeDtypeStruct((DEPTH, SUBLANES, 3 * D_MODEL), F32),
        compiler_params=pltpu.CompilerParams(vmem_limit_bytes=VMEM_LIMIT),
        name="mod_vectors",
    )(mvecs, w_mod, b_mod.reshape(DEPTH, 1, 3 * D_MODEL))


def _modulated_norm(x, g, mod):
    return _rms(x) * g * (1.0 + mod[1:2, :]) + mod[0:1, :]


def _inproj_kernel(x_ref, mod_ref, g_ref, w_ref, o_ref, h_ref):
    @pl.when(pl.program_id(1) == 0)
    def _():
        h_ref[...] = _modulated_norm(x_ref[...], g_ref[...], mod_ref[...]).astype(BF16)

    o_ref[...] = _dot(h_ref[...], w_ref[...])


def _in_projection(x, mods, mod_row, g, w1, tm):
    t = x.shape[0]
    return pl.pallas_call(
        _inproj_kernel,
        grid=(t // tm, MIX_COLS // TN_IN),
        in_specs=[
            pl.BlockSpec((tm, D_MODEL), lambda i, j: (i, 0)),
            pl.BlockSpec((None, 3, D_MODEL), lambda i, j: (mod_row(i), 0, 0)),
            pl.BlockSpec((1, D_MODEL), lambda i, j: (0, 0)),
            pl.BlockSpec((D_MODEL, TN_IN), lambda i, j: (0, j)),
        ],
        out_specs=pl.BlockSpec((tm, TN_IN), lambda i, j: (i, j)),
        out_shape=jax.ShapeDtypeStruct((t, MIX_COLS), F32),
        scratch_shapes=[pltpu.VMEM((tm, D_MODEL), BF16)],
        compiler_params=pltpu.CompilerParams(
            dimension_semantics=("parallel", "arbitrary"), vmem_limit_bytes=VMEM_LIMIT),
        name="in_projection",
    )(x, mods, g, w1)


def _merge_kernel(x_ref, mod_ref, g_ref, oa_ref, ob_ref, oc_ref, od_ref, wz_ref, wg_ref,
                  wb_ref, wo_ref, *rest, final):
    if final:
        fg_ref, o_ref, y_ref = rest
    else:
        (o_ref,) = rest
    x = x_ref[...]
    mod = mod_ref[...]
    h = _modulated_norm(x, g_ref[...], mod).astype(BF16)
    acc = None
    for k, b_ref in enumerate((oa_ref, ob_ref, oc_ref, od_ref)):
        z = _dot(h, wz_ref[k])
        br = (b_ref[...] * _silu(z)).astype(BF16)
        yb = _dot(br, wb_ref[k])
        term = jax.nn.sigmoid(_dot(h, wg_ref[k])) * yb
        acc = term if acc is None else acc + term
    y = _dot(acc.astype(BF16), wo_ref[...])
    out = x + mod[2:3, :] * y
    o_ref[...] = out
    if final:
        y_ref[...] = _rms(out) * fg_ref[...]


def _merge(x, mods, mod_row, g, outs, wz, wg, wb, wo, final_g, tm):
    t = x.shape[0]
    final = final_g is not None
    const = dict(pipeline_mode=pl.Buffered(1))
    in_specs = [
        pl.BlockSpec((tm, D_MODEL), lambda i: (i, 0)),
        pl.BlockSpec((None, 3, D_MODEL), lambda i: (mod_row(i), 0, 0)),
        pl.BlockSpec((1, D_MODEL), lambda i: (0, 0)),
    ] + [pl.BlockSpec((tm, MIX_W), lambda i: (i, 0)) for _ in range(N_BRANCH)] + [
        pl.BlockSpec((N_BRANCH, D_MODEL, MIX_W), lambda i: (0, 0, 0), **const),
        pl.BlockSpec((N_BRANCH, D_MODEL, D_MODEL), lambda i: (0, 0, 0), **const),
        pl.BlockSpec((N_BRANCH, MIX_W, D_MODEL), lambda i: (0, 0, 0), **const),
        pl.BlockSpec((D_MODEL, D_MODEL), lambda i: (0, 0), **const),
    ]
    args = [x, mods, g, *outs, wz, wg, wb, wo]
    out_specs = [pl.BlockSpec((tm, D_MODEL), lambda i: (i, 0))]
    out_shape = [jax.ShapeDtypeStruct((t, D_MODEL), F32)]
    if final:
        in_specs.append(pl.BlockSpec((1, D_MODEL), lambda i: (0, 0)))
        args.append(final_g)
        out_specs.append(pl.BlockSpec((tm, D_MODEL), lambda i: (i, 0)))
        out_shape.append(jax.ShapeDtypeStruct((t, D_MODEL), F32))
    res = pl.pallas_call(
        functools.partial(_merge_kernel, final=final),
        grid=(t // tm,),
        in_specs=in_specs,
        out_specs=out_specs,
        out_shape=out_shape,
        compiler_params=pltpu.CompilerParams(
            dimension_semantics=("parallel",), vmem_limit_bytes=VMEM_LIMIT),
        name="merge",
    )(*args)
    return res if final else (res[0], None)


def _state_specs(layer):
    s_in = pl.BlockSpec((None, None, 2, None, D_HEAD, D_HEAD), lambda b, h: (b, layer, 0, h, 0, 0))
    s_out = pl.BlockSpec((None, 2, None, D_HEAD, D_HEAD), lambda b, h: (b, 0, h, 0, 0))
    return s_in, s_out


def _col_spec(n_tok, cb):
    return pl.BlockSpec((n_tok, LANES), lambda b, h: (b, cb + h))


def _head_mixer_call(kernel, name, bsz, n_tok, in_specs, args, scratch=()):
    _, s_out = _state_specs(0)
    return pl.pallas_call(
        kernel,
        grid=(bsz, N_HEAD),
        in_specs=in_specs,
        out_specs=[pl.BlockSpec((n_tok, LANES), lambda b, h: (b, h)), s_out],
        out_shape=[jax.ShapeDtypeStruct((bsz * n_tok, MIX_W), F32),
                   jax.ShapeDtypeStruct((bsz, 2, N_HEAD, D_HEAD, D_HEAD), F32)],
        scratch_shapes=list(scratch),
        compiler_params=pltpu.CompilerParams(
            dimension_semantics=("parallel", "arbitrary"), vmem_limit_bytes=VMEM_LIMIT),
        name=name,
    )(*args)


def _zero_rows(o_ref, n_tok):
    def body(c, carry):
        o_ref[_rows(c), :] = jnp.zeros((CHUNK, LANES), F32)
        return carry

    lax.fori_loop(0, n_tok // CHUNK, body, 0)


def _norm_rows(o_ref, n_tok, g_ref):
    def body(c, carry):
        o_ref[_rows(c), :] = _rms(o_ref[_rows(c), :]) * g_ref[...]
        return carry

    lax.fori_loop(0, n_tok // CHUNK, body, 0)


def _ret_kernel(lg_ref, q_ref, k_ref, v_ref, *rest, n_tok, rope, has_init):
    rest = list(rest)
    cos_ref, sin_ref = (rest.pop(0), rest.pop(0)) if rope else (None, None)
    s0_ref = rest.pop(0) if has_init else None
    o_ref, sf_ref, kvb_ref = rest
    h = pl.program_id(1)
    size = RET_CHUNK
    nc = n_tok // size
    lg_f, lg_b = lg_ref[0, h], lg_ref[1, h]
    pos_c = lax.broadcasted_iota(jnp.int32, (size, 1), 0).astype(F32)
    pos_r = lax.broadcasted_iota(jnp.int32, (1, size), 1).astype(F32)
    rel = pos_c - pos_r
    lower, upper = rel >= 0, rel <= 0
    dsum = (jnp.where(lower, jnp.exp(jnp.where(lower, rel, 0.0) * lg_f), 0.0)
            + jnp.where(upper, jnp.exp(jnp.where(upper, -rel, 0.0) * lg_b), 0.0))
    q_dec_f = jnp.exp((pos_c + 1.0) * lg_f)
    k_dec_f = jnp.exp((size - 1.0 - pos_c) * lg_f)
    q_dec_b = jnp.exp((size - pos_c) * lg_b)
    k_dec_b = jnp.exp(pos_c * lg_b)
    c_dec_f = jnp.exp(jnp.full((1, D_HEAD), size * lg_f, F32))
    c_dec_b = jnp.exp(jnp.full((1, D_HEAD), size * lg_b, F32))

    def rotate(x, rows):
        if not rope:
            return x
        return x * cos_ref[rows, :] + pltpu.roll(x, D_HEAD // 2, 1) * sin_ref[rows, :]

    unroll = 2 if nc % 2 == 0 else 1

    def forward(i, s):
        cs = [i * unroll + j for j in range(unroll)]
        rows = [_rows(c, size) for c in cs]
        q = [rotate(q_ref[r, :], r) for r in rows]
        k = [rotate(k_ref[r, :], r) * D_HEAD ** -0.5 for r in rows]
        v = [v_ref[r, :] for r in rows]
        scores = [_dot_nt(qj, kj) for qj, kj in zip(q, k)]
        kv_b = [_dot_tn(kj * k_dec_b, vj) for kj, vj in zip(k, v)]
        kv_f = [_dot_tn(kj * k_dec_f, vj) for kj, vj in zip(k, v)]
        inter = []
        for j in range(unroll):
            inter.append(_dot(q[j] * q_dec_f, s))
            s = s * c_dec_f + kv_f[j]
        for j in range(unroll):
            o_ref[rows[j], :] = _dot(scores[j] * dsum, v[j]) + inter[j]
            kvb_ref[cs[j]] = kv_b[j]
        return s

    s0 = s0_ref[0] if has_init else jnp.zeros((D_HEAD, D_HEAD), F32)
    sf_ref[0] = lax.fori_loop(0, nc // unroll, forward, s0)

    def backward(i, s):
        for j in range(unroll):
            c = nc - 1 - (i * unroll + j)
            rows = _rows(c, size)
            q = rotate(q_ref[rows, :], rows)
            o_ref[rows, :] = _rms(o_ref[rows, :] + _dot(q * q_dec_b, s))
            s = s * c_dec_b + kvb_ref[c]
        return s

    s0 = s0_ref[1] if has_init else jnp.zeros((D_HEAD, D_HEAD), F32)
    sf_ref[1] = lax.fori_loop(0, nc // unroll, backward, s0)


def _retention(proj, log_gamma, rope, state, layer, bsz, n_tok):
    in_specs = [pl.BlockSpec(memory_space=pltpu.SMEM),
                _col_spec(n_tok, CB_AQ), _col_spec(n_tok, CB_AK), _col_spec(n_tok, CB_AV)]
    args = [log_gamma, proj, proj, proj]
    if rope is not None:
        in_specs += [pl.BlockSpec((n_tok, LANES), lambda b, h: (0, 0))] * 2
        args += list(rope)
    if state is not None:
        in_specs.append(_state_specs(layer)[0])
        args.append(state)
    kernel = functools.partial(_ret_kernel, n_tok=n_tok, rope=rope is not None,
                               has_init=state is not None)
    scratch = [pltpu.VMEM((n_tok // RET_CHUNK, D_HEAD, D_HEAD), F32)]
    return _head_mixer_call(kernel, "retention", bsz, n_tok, in_specs, args, scratch)


def _tile_scan(a, u, reverse):
    row = lax.broadcasted_iota(jnp.int32, a.shape, 0)
    for s in (1, 2, 4):
        if reverse:
            valid = row < SUBLANES - s
            shift = SUBLANES - s
        else:
            valid = row >= s
            shift = s
        a_sh = jnp.where(valid, pltpu.roll(a, shift, 0), 1.0)
        u_sh = jnp.where(valid, pltpu.roll(u, shift, 0), 0.0)
        u = a * u_sh + u
        a = a * a_sh
    return a, u


def _lru_kernel(x_ref, cw_ref, cb_ref, gw_ref, gb_ref, lam_ref, *rest, n_tok, has_init):
    rest = list(rest)
    h0_ref = rest.pop(0) if has_init else None
    o_ref, hl_ref = rest
    nc = n_tok // CHUNK
    n_tile = CHUNK // SUBLANES
    cw = cw_ref[...]
    cb = cb_ref[...]
    _zero_rows(o_ref, n_tok)

    def body(c, carry):
        ccs = (c, nc - 1 - c)
        xb = [_conv_chunk(x_ref, cc, nc, n_tok, cw) + cb for cc in ccs]
        pre = [(_dot(xb[d], gw_ref[d, 0]), _dot(xb[d], gw_ref[d, 1])) for d in range(2)]
        a, u = [], []
        for d in range(2):
            coef = -LRU_C * jax.nn.softplus(-lam_ref[d])
            r = jax.nn.sigmoid(pre[d][0] + gb_ref[2 * d])
            i = jax.nn.sigmoid(pre[d][1] + gb_ref[2 * d + 1])
            log_a = coef * r
            a_d = jnp.exp(log_a)
            one_minus_a2 = -jnp.tanh(log_a) * (a_d * a_d + 1.0)
            a.append(a_d)
            u.append(jnp.sqrt(jnp.maximum(one_minus_a2, 1e-12)) * (i * xb[d]))
        carry = list(carry)
        tiles = [[None] * n_tile, [None] * n_tile]
        for step in range(n_tile):
            for d in range(2):
                t = n_tile - 1 - step if d == 1 else step
                last = 0 if d == 1 else SUBLANES - 1
                sl = slice(t * SUBLANES, (t + 1) * SUBLANES)
                a_t, u_t = _tile_scan(a[d][sl, :], u[d][sl, :], d == 1)
                h_t = a_t * carry[d] + u_t
                carry[d] = h_t[last:last + 1, :]
                tiles[d][t] = h_t
        for d in range(2):
            rows = _rows(ccs[d])
            o_ref[rows, :] = o_ref[rows, :] + jnp.concatenate(tiles[d], axis=0)
        return tuple(carry)

    zero = jnp.zeros((1, LANES), F32)
    h0 = (h0_ref[0], h0_ref[1]) if has_init else (zero, zero)
    h_f, h_b = lax.fori_loop(0, nc, body, h0)
    hl_ref[0] = h_f
    hl_ref[1] = h_b


def _rglru(proj, conv_w, conv_b, gate_w, gate_b, lam, state, layer, bsz, n_tok):
    n_pair = MIX_W // LANES
    in_specs = [
        pl.BlockSpec((n_tok, LANES), lambda b, p: (b, CB_BX + p)),
        pl.BlockSpec((CONV_K, LANES), lambda b, p: (0, p)),
        pl.BlockSpec((1, LANES), lambda b, p: (0, p)),
        pl.BlockSpec((2, 2, None, LANES, LANES), lambda b, p: (0, 0, p, 0, 0)),
        pl.BlockSpec((4, 1, LANES), lambda b, p: (0, 0, p)),
        pl.BlockSpec((2, 1, LANES), lambda b, p: (0, 0, p)),
    ]
    args = [proj, conv_w, conv_b, gate_w, gate_b, lam]
    if state is not None:
        in_specs.append(pl.BlockSpec((None, None, 2, 1, LANES), lambda b, p: (b, layer, 0, 0, p)))
        args.append(state)
    return pl.pallas_call(
        functools.partial(_lru_kernel, n_tok=n_tok, has_init=state is not None),
        grid=(bsz, n_pair),
        in_specs=in_specs,
        out_specs=[pl.BlockSpec((n_tok, LANES), lambda b, p: (b, p)),
                   pl.BlockSpec((None, 2, 1, LANES), lambda b, p: (b, 0, 0, p))],
        out_shape=[jax.ShapeDtypeStruct((bsz * n_tok, MIX_W), F32),
                   jax.ShapeDtypeStruct((bsz, 2, 1, MIX_W), F32)],
        compiler_params=pltpu.CompilerParams(
            dimension_semantics=("parallel", "arbitrary"), vmem_limit_bytes=VMEM_LIMIT),
        name="rglru",
    )(*args)


def _unit_lower_inverses(mats):
    i = lax.broadcasted_iota(jnp.int32, (CHUNK, CHUNK), 0)
    j = lax.broadcasted_iota(jnp.int32, (CHUNK, CHUNK), 1)
    eye = jnp.where(i == j, 1.0, 0.0)
    ps = [-a for a in mats]
    ts = [eye + p for p in ps]
    steps = CHUNK.bit_length() - 2
    for _ in range(steps):
        halves = [_split2(p) for p in ps]
        ps = [_dot_split(hp, hp) for hp in halves]
        halves = [_split2(p) for p in ps]
        ts = [t + _dot_split(_split2(t), hp) for t, hp in zip(ts, halves)]
    return ts


def _gdn_kernel(na_ref, dtb_ref, q_ref, k_ref, v_ref, ab_ref, wq_ref, wk_ref, wv_ref, g_ref,
                *rest, n_tok, has_init):
    rest = list(rest)
    s0_ref = rest.pop(0) if has_init else None
    o_ref, sf_ref, qs_ref, ks_ref, vs_ref = rest
    h = pl.program_id(1)
    nc = n_tok // CHUNK

    def prep(c, carry):
        rows = _rows(c)
        q = _silu(_conv_chunk(q_ref, c, nc, n_tok, wq_ref[...]))
        k = _silu(_conv_chunk(k_ref, c, nc, n_tok, wk_ref[...]))
        v = _silu(_conv_chunk(v_ref, c, nc, n_tok, wv_ref[...]))
        q = q * lax.rsqrt(jnp.sum(q * q, axis=-1, keepdims=True) + EPS)
        k = k * lax.rsqrt(jnp.sum(k * k, axis=-1, keepdims=True) + EPS)
        qs_ref[rows, :] = q * D_HEAD ** -0.5
        ks_ref[rows, :] = k
        vs_ref[rows, :] = v
        o_ref[rows, :] = jnp.zeros((CHUNK, LANES), F32)
        return carry

    lax.fori_loop(0, nc, prep, 0)

    lane = lax.broadcasted_iota(jnp.int32, (CHUNK, LANES), 1)

    def body(i, carry):
        ctx = [(d, (nc - 1 - (i * GROUP + j)) if d == 1 else i * GROUP + j)
               for j in range(GROUP) for d in range(2)]
        n = len(ctx)
        masks = {d: (_order_mask(CHUNK, d == 1, strict=False), _order_mask(CHUNK, d == 1, strict=True))
                 for d in range(2)}
        rows = [_rows(cc) for _, cc in ctx]
        q = [qs_ref[r, :] for r in rows]
        k = [ks_ref[r, :] for r in rows]
        v = [vs_ref[r, :] for r in rows]
        beta, both = [], []
        for (d, _), r in zip(ctx, rows):
            incl, strict = masks[d]
            ab = ab_ref[r, :]
            a_in = jnp.sum(jnp.where(lane == d * N_HEAD + h, ab, 0.0), axis=1, keepdims=True)
            b_in = jnp.sum(jnp.where(lane == (2 + d) * N_HEAD + h, ab, 0.0), axis=1, keepdims=True)
            g = na_ref[d, h] * jax.nn.softplus(a_in + dtb_ref[d, h])
            beta.append(jax.nn.sigmoid(b_in))
            rhs = jnp.concatenate([jnp.broadcast_to(g, (CHUNK, CHUNK)), g * strict.astype(F32)], axis=1)
            both.append(_dot_mask(incl.astype(BF16), rhs))
        qk_kk = [_dot_nt(jnp.concatenate([q[m], k[m]], axis=0), k[m]) for m in range(n)]
        gc, gam, a_mat = [], [], []
        for m, (d, _) in enumerate(ctx):
            incl, strict = masks[d]
            gc.append(jnp.broadcast_to(both[m][:, 0:1], (CHUNK, LANES)))
            gam.append(jnp.where(incl, jnp.exp(jnp.where(incl, both[m][:, CHUNK:], 0.0)), 0.0))
            a_mat.append(jnp.where(strict, beta[m] * qk_kk[m][CHUNK:] * gam[m], 0.0))
        t_inv = _unit_lower_inverses(a_mat)
        e_gc = [jnp.exp(x) for x in gc]
        wu = [_dot_split(_split2(t_inv[m]),
                         _split2(jnp.concatenate([k[m] * (beta[m] * e_gc[m]), v[m] * beta[m]], axis=1)))
              for m in range(n)]
        a_wu = [_dot(qk_kk[m][:CHUNK] * gam[m], wu[m]) for m in range(n)]
        gl = [gc[m][(0 if d == 1 else CHUNK - 1):(1 if d == 1 else CHUNK), :] for m, (d, _) in enumerate(ctx)]
        kd_wu = [_dot_tn(k[m] * jnp.exp(gl[m] - gc[m]), wu[m]) for m in range(n)]
        s = list(carry)
        for m, (d, _) in enumerate(ctx):
            qt = q[m] * e_gc[m] - a_wu[m][:, :D_HEAD]
            o = a_wu[m][:, D_HEAD:] + _dot(qt, s[d])
            o_ref[rows[m], :] = o_ref[rows[m], :] + o
            s[d] = s[d] * jnp.exp(gl[m]) - _dot(kd_wu[m][:, :D_HEAD], s[d]) + kd_wu[m][:, D_HEAD:]
        return tuple(s)

    zero = jnp.zeros((D_HEAD, D_HEAD), F32)
    s0 = (s0_ref[0], s0_ref[1]) if has_init else (zero, zero)
    s_f, s_b = lax.fori_loop(0, nc // GROUP, body, s0)
    sf_ref[0] = s_f
    sf_ref[1] = s_b
    _norm_rows(o_ref, n_tok, g_ref)


def _gated_delta(proj, neg_a, dt_bias, conv_w, norm_g, state, layer, bsz, n_tok):
    smem = pl.BlockSpec(memory_space=pltpu.SMEM)
    in_specs = [smem, smem,
                _col_spec(n_tok, CB_CQ), _col_spec(n_tok, CB_CK), _col_spec(n_tok, CB_CV),
                pl.BlockSpec((n_tok, LANES), lambda b, h: (b, CB_CAB)),
                pl.BlockSpec((CONV_K, LANES), lambda b, h: (0, h)),
                pl.BlockSpec((CONV_K, LANES), lambda b, h: (0, N_HEAD + h)),
                pl.BlockSpec((CONV_K, LANES), lambda b, h: (0, 2 * N_HEAD + h)),
                pl.BlockSpec((1, D_HEAD), lambda b, h: (0, 0))]
    args = [neg_a, dt_bias, proj, proj, proj, proj, conv_w, conv_w, conv_w, norm_g]
    if state is not None:
        in_specs.append(_state_specs(layer)[0])
        args.append(state)
    kernel = functools.partial(_gdn_kernel, n_tok=n_tok, has_init=state is not None)
    scratch = [pltpu.VMEM((n_tok, D_HEAD), F32)] * 3
    return _head_mixer_call(kernel, "gated_delta", bsz, n_tok, in_specs, args, scratch)


def _block_rows(x, row):
    nb = CHUNK // SUB
    x3 = x.reshape(nb, SUB, LANES)[:, row:row + 1, :]
    return jnp.broadcast_to(x3, (nb, SUB, LANES)).reshape(CHUNK, LANES)


def _hgrn_kernel(q_ref, f0_ref, f1_ref, v_ref, lb_ref, g_ref, *rest, n_tok, has_init):
    rest = list(rest)
    s0_ref = rest.pop(0) if has_init else None
    o_ref, sf_ref = rest
    nc = n_tok // CHUNK
    nb = CHUNK // SUB
    blk_c = lax.broadcasted_iota(jnp.int32, (CHUNK, 1), 0) // SUB
    blk_i = lax.broadcasted_iota(jnp.int32, (CHUNK, CHUNK), 0) // SUB
    blk_j = lax.broadcasted_iota(jnp.int32, (CHUNK, CHUNK), 1) // SUB
    _zero_rows(o_ref, n_tok)

    def body(i, carry):
        ctx = [(d, (nc - 1 - (i * GROUP_D + j)) if d == 1 else i * GROUP_D + j)
               for j in range(GROUP_D) for d in range(2)]
        n = len(ctx)
        incl = {d: _order_mask(CHUNK, d == 1, strict=False) for d in range(2)}
        rows = [_rows(cc) for _, cc in ctx]
        q = [_silu(q_ref[r, :]) * D_HEAD ** -0.5 for r in rows]
        v = [v_ref[r, :] for r in rows]
        k, log_f, bc = [], [], []
        for (d, _), r in zip(ctx, rows):
            zf = (f1_ref if d == 1 else f0_ref)[r, :]
            lb = lb_ref[d:d + 1, :]
            k.append((1.0 - lb) * jax.nn.sigmoid(-zf))
            f_gate = lb + (1.0 - lb) * jax.nn.sigmoid(zf)
            log_f.append(jnp.log(jnp.maximum(f_gate, 1e-30)))
            bc.append(_dot_mask(incl[d].astype(BF16), log_f[-1]))
        ex = [b - lf for b, lf in zip(bc, log_f)]
        a_diag = []
        for m, (d, _) in enumerate(ctx):
            m_blk = _block_rows(bc[m], SUB // 2 if d == 1 else SUB // 2 - 1)
            a_diag.append(_dot_nt(q[m] * jnp.exp(bc[m] - m_blk), k[m] * jnp.exp(m_blk - bc[m])))
        a_mat = []
        for m, (d, _) in enumerate(ctx):
            first = SUB - 1 if d == 1 else 0
            q_c = q[m] * jnp.exp(bc[m] - _block_rows(ex[m], first))
            cross = []
            for blk in range(nb):
                if blk == (nb - 1 if d == 1 else 0):
                    cross.append(jnp.zeros((SUB, CHUNK), F32))
                    continue
                earlier = (blk_c > blk) if d == 1 else (blk_c < blk)
                r_i = ex[m][blk * SUB + first:blk * SUB + first + 1, :]
                k_t = jnp.where(earlier, k[m] * jnp.exp(jnp.where(earlier, r_i - bc[m], 0.0)), 0.0)
                cross.append(_dot_nt(q_c[blk * SUB:(blk + 1) * SUB, :], k_t))
            diag = incl[d] & (blk_i == blk_j)
            a_mat.append(jnp.where(diag, a_diag[m], 0.0) + jnp.concatenate(cross, axis=0))
        b_last = [bc[m][(0 if d == 1 else CHUNK - 1):(1 if d == 1 else CHUNK), :]
                  for m, (d, _) in enumerate(ctx)]
        kv = [_dot_tn(v[m], k[m] * jnp.exp(b_last[m] - bc[m])) for m in range(n)]
        intra = [_dot(a_mat[m], v[m]) for m in range(n)]
        st = list(carry)
        for m, (d, _) in enumerate(ctx):
            o = intra[m] + _dot_nt(q[m] * jnp.exp(bc[m]), st[d])
            o_ref[rows[m], :] = o_ref[rows[m], :] + o
            st[d] = st[d] * jnp.exp(b_last[m]) + kv[m]
        return tuple(st)

    zero = jnp.zeros((D_HEAD, D_HEAD), F32)
    st0 = (s0_ref[0].T, s0_ref[1].T) if has_init else (zero, zero)
    st_f, st_b = lax.fori_loop(0, nc // GROUP_D, body, st0)
    sf_ref[0] = st_f.T
    sf_ref[1] = st_b.T
    _norm_rows(o_ref, n_tok, g_ref)


def _hgrn2(proj, lower_bound, norm_g, state, layer, bsz, n_tok):
    in_specs = [_col_spec(n_tok, CB_DQ), _col_spec(n_tok, CB_DF), _col_spec(n_tok, CB_DF + N_HEAD),
                _col_spec(n_tok, CB_DI),
                pl.BlockSpec((2, D_HEAD), lambda b, h: (0, h)),
                pl.BlockSpec((1, D_HEAD), lambda b, h: (0, 0))]
    args = [proj, proj, proj, proj, lower_bound, norm_g]
    if state is not None:
        in_specs.append(_state_specs(layer)[0])
        args.append(state)
    kernel = functools.partial(_hgrn_kernel, n_tok=n_tok, has_init=state is not None)
    return _head_mixer_call(kernel, "hgrn2", bsz, n_tok, in_specs, args)


def _rope_tables(n_tok):
    n_freq = D_HEAD // 4
    inv = ROPE_BASE ** (-jnp.arange(n_freq, dtype=F32) / n_freq)
    rows = n_tok // GRID_W
    r = jnp.repeat(jnp.arange(rows, dtype=F32), GRID_W)
    col = jnp.tile(jnp.arange(GRID_W, dtype=F32), rows)
    ang = jnp.concatenate([r[:, None] * inv, col[:, None] * inv], axis=-1)
    cos, sin = jnp.cos(ang), jnp.sin(ang)
    return jnp.concatenate([cos, cos], axis=-1), jnp.concatenate([-sin, sin], axis=-1)


def _split_w_in(w):
    m = MIX_W
    a = w[:, 0:4 * m]
    b = w[:, 4 * m:6 * m]
    c = w[:, 6 * m:10 * m + 16]
    dd = w[:, 10 * m + 16:15 * m + 16]
    mg = w[:, 15 * m + 16:]
    cab = jnp.pad(c[:, 4 * m:], ((0, 0), (0, LANES - 16)))
    w1 = jnp.concatenate([a[:, :3 * m], b[:, :m], c[:, :3 * m], dd[:, :4 * m], cab], axis=1)
    wz = jnp.stack([a[:, 3 * m:], b[:, m:], c[:, 3 * m:4 * m], dd[:, 4 * m:]], axis=0)
    wg = jnp.moveaxis(mg.reshape(D_MODEL, N_BRANCH, D_MODEL), 1, 0)
    return w1.astype(BF16), wz.astype(BF16), wg.astype(BF16)


def _pair_block_diag(gw):
    z = jnp.zeros_like(gw[:, :, 0::2])
    top = jnp.concatenate([gw[:, :, 0::2], z], axis=-1)
    bot = jnp.concatenate([z, gw[:, :, 1::2]], axis=-1)
    return jnp.concatenate([top, bot], axis=-2)


def kernel(x_prompt, x_sample, state_ret, state_lru, state_gdn, state_hgrn, c, c_ctx, norm_g,
           w_mod, b_mod, w_in, ret_decay, lru_conv_w, lru_conv_b, lru_gate_w, lru_gate_b,
           lru_lambda, gdn_conv_w, gdn_a_log, gdn_dt_bias, gdn_norm_g, hgrn_lb, hgrn_norm_g,
           w_branch, w_out, final_norm_g):
    bp, n_ctx, _ = x_prompt.shape
    bs, n_lat, _ = x_sample.shape

    lb_sm = jax.nn.softmax(hgrn_lb.astype(F32), axis=0)
    lower_bounds = jnp.cumsum(lb_sm, axis=0) - lb_sm[0]
    log_gamma = jnp.log1p(-jnp.exp(ret_decay.astype(F32)))
    gdn_neg_a = -jnp.exp(gdn_a_log.astype(F32))
    rope = _rope_tables(n_lat)
    state_lru5 = state_lru.reshape(bs, DEPTH, 2, 1, MIX_W)
    layer_w = [(*_split_w_in(w_in[l]), w_branch[l].astype(BF16), w_out[l].astype(BF16),
                _pair_block_diag(lru_gate_w[l])) for l in range(DEPTH)]

    mvecs = jnp.concatenate([c_ctx[None, :], c, jnp.zeros((SUBLANES - 1 - bs, D_MODEL), F32)], axis=0)
    mods = _mod_vectors(mvecs, w_mod, b_mod).reshape(DEPTH, SUBLANES, 3, D_MODEL)

    tm = 512
    groups = [
        dict(x=x_prompt.reshape(bp * n_ctx, D_MODEL), bsz=bp, n_tok=n_ctx, rope=None, cached=False,
             mod_row=lambda i: 0),
        dict(x=x_sample.reshape(bs * n_lat, D_MODEL), bsz=bs, n_tok=n_lat, rope=rope, cached=True,
             mod_row=lambda i: 1 + (i * tm) // n_lat),
    ]
    new_states = [[], [], [], []]
    finals = []
    for grp in groups:
        x, bsz, n_tok = grp["x"], grp["bsz"], grp["n_tok"]
        cached = grp["cached"]
        for l in range(DEPTH):
            w1, wz, wg, wb, wo, gate_w = layer_w[l]
            g = norm_g[l][None, :]
            proj = _in_projection(x, mods[l], grp["mod_row"], g, w1, tm)
            o_a, s_a = _retention(proj, log_gamma[l], grp["rope"], state_ret if cached else None,
                                  l, bsz, n_tok)
            gate_b = lru_gate_b[l].reshape(4, 1, MIX_W)
            o_b, s_b = _rglru(proj, lru_conv_w[l], lru_conv_b[l][None, :], gate_w, gate_b,
                              lru_lambda[l][:, None, :], state_lru5 if cached else None, l, bsz, n_tok)
            o_c, s_c = _gated_delta(proj, gdn_neg_a[l], gdn_dt_bias[l].astype(F32), gdn_conv_w[l],
                                    gdn_norm_g[l][None, :], state_gdn if cached else None, l, bsz, n_tok)
            o_d, s_d = _hgrn2(proj, lower_bounds[l], hgrn_norm_g[l][None, :],
                              state_hgrn if cached else None, l, bsz, n_tok)
            final_g = final_norm_g[None, :] if l == DEPTH - 1 else None
            x, y = _merge(x, mods[l], grp["mod_row"], g, (o_a, o_b, o_c, o_d), wz, wg, wb, wo, final_g, tm)
            if not cached:
                for acc, s in zip(new_states, (s_a, s_b, s_c, s_d)):
                    acc.append(s)
        finals.append(y)

    y_prompt = finals[0].reshape(bp, n_ctx, D_MODEL)
    y_sample = finals[1].reshape(bs, n_lat, D_MODEL)
    new_ret = jnp.stack(new_states[0], axis=1)
    new_lru = jnp.stack(new_states[1], axis=1).reshape(bp, DEPTH, 2, MIX_W)
    new_gdn = jnp.stack(new_states[2], axis=1)
    new_hgrn = jnp.stack(new_states[3], axis=1)
    return (y_prompt, y_sample, new_ret, new_lru, new_gdn, new_hgrn)
```

```python
import functools

import jax
import jax.numpy as jnp
from jax import lax
from jax.experimental import pallas as pl
from jax.experimental.pallas import tpu as pltpu

F32 = jnp.float32
BF16 = jnp.bfloat16

D_MODEL = 1024
DEPTH = 4
MIX_W = 512
N_BRANCH = 4
N_HEAD = 4
D_HEAD = 128
GRID_W = 64
LRU_C = 8.0
CONV_K = 4
CONV_LEFT = 2
CHUNK = 64
RET_CHUNK = 256
SUB = 16
GROUP = 8
GROUP_D = 4
PACK = 4
NEUMANN_SPLIT_STEPS = 3
ROPE_BASE = 10000.0
EPS = 1e-6
LANES = 128
SUBLANES = 8
VMEM_LIMIT = 56 * 1024 * 1024

CB_AQ, CB_AK, CB_AV, CB_BX, CB_CQ, CB_CK, CB_CV, CB_DQ, CB_DF, CB_DI, CB_CAB = (
    0, 4, 8, 12, 16, 20, 24, 28, 32, 40, 44)
MIX_COLS = 45 * LANES
TN_IN = 1920


def _dot(a, b):
    return jnp.dot(a, b, preferred_element_type=F32)


def _dot_nt(a, b):
    return lax.dot_general(a, b, (((1,), (1,)), ((), ())), preferred_element_type=F32)


def _dot_tn(a, b):
    return lax.dot_general(a, b, (((0,), (0,)), ((), ())), preferred_element_type=F32)


def _split2(x):
    hi = x.astype(BF16)
    lo = (x - hi.astype(F32)).astype(BF16)
    return hi, lo


def _dot_split(a, b):
    return _dot(a[0], b[0]) + (_dot(a[0], b[1]) + _dot(a[1], b[0]))


def _dot_mask(mask_bf16, x):
    hi = x.astype(BF16)
    r1 = x - hi.astype(F32)
    mid = r1.astype(BF16)
    lo = (r1 - mid.astype(F32)).astype(BF16)
    return _dot(mask_bf16, hi) + (_dot(mask_bf16, mid) + _dot(mask_bf16, lo))


def _order_mask(n, reverse, strict):
    i = lax.broadcasted_iota(jnp.int32, (n, n), 0)
    j = lax.broadcasted_iota(jnp.int32, (n, n), 1)
    if reverse:
        return (j > i) if strict else (j >= i)
    return (j < i) if strict else (j <= i)


def _rows(cc, size=CHUNK):
    return pl.ds(pl.multiple_of(cc * size, size), size)


def _conv_chunk(x_ref, cc, nc, n_tok, w):
    r0 = pl.multiple_of(cc * CHUNK, CHUNK)
    cur = x_ref[pl.ds(r0, CHUNK), :]
    p0 = pl.multiple_of(jnp.maximum(r0 - SUBLANES, 0), SUBLANES)
    n0 = pl.multiple_of(jnp.minimum(r0 + CHUNK, n_tok - SUBLANES), SUBLANES)
    prev = x_ref[pl.ds(p0, SUBLANES), :] * jnp.where(cc > 0, 1.0, 0.0)
    nxt = x_ref[pl.ds(n0, SUBLANES), :] * jnp.where(cc < nc - 1, 1.0, 0.0)
    xw = jnp.concatenate([prev, cur, nxt], axis=0)
    rows = CHUNK + 2 * SUBLANES
    acc = None
    for j in range(CONV_K):
        shift = (CONV_LEFT - j) % rows
        rolled = xw if shift == 0 else pltpu.roll(xw, shift, 0)
        term = rolled[SUBLANES:SUBLANES + CHUNK, :] * w[j:j + 1, :]
        acc = term if acc is None else acc + term
    return acc


def _silu(x):
    return x * jax.nn.sigmoid(x)


def _rms(x):
    return x * lax.rsqrt(jnp.mean(x * x, axis=-1, keepdims=True) + EPS)


def _mod_kernel(m_ref, w_ref, b_ref, o_ref):
    m = _silu(m_ref[...])
    o_ref[...] = _dot(m.astype(BF16), w_ref[...].astype(BF16)) + b_ref[...]


def _mod_vectors(mvecs, w_mod, b_mod):
    tn = D_MODEL
    return pl.pallas_call(
        _mod_kernel,
        grid=(DEPTH, 3 * D_MODEL // tn),
        in_specs=[
            pl.BlockSpec((SUBLANES, D_MODEL), lambda l, j: (0, 0)),
            pl.BlockSpec((None, D_MODEL, tn), lambda l, j: (l, 0, j)),
            pl.BlockSpec((None, 1, tn), lambda l, j: (l, 0, j)),
        ],
        out_specs=pl.BlockSpec((None, SUBLANES, tn), lambda l, j: (l, 0, j)),
        out_shape=jax.ShapeDtypeStruct((DEPTH, SUBLANES, 3 * D_MODEL), F32),
        compiler_params=pltpu.CompilerParams(vmem_limit_bytes=VMEM_LIMIT),
        name="mod_vectors",
    )(mvecs, w_mod, b_mod.reshape(DEPTH, 1, 3 * D_MODEL))


def _modulated_norm(x, g, mod):
    return _rms(x) * g * (1.0 + mod[1:2, :]) + mod[0:1, :]


def _inproj_kernel(x_ref, mod_ref, g_ref, w_ref, o_ref, h_ref):
    @pl.when(pl.program_id(1) == 0)
    def _():
        h_ref[...] = _modulated_norm(x_ref[...], g_ref[...], mod_ref[...]).astype(BF16)

    o_ref[...] = _dot(h_ref[...], w_ref[...])


def _in_projection(x, mods, mod_row, g, w1, tm):
    t = x.shape[0]
    return pl.pallas_call(
        _inproj_kernel,
        grid=(t // tm, MIX_COLS // TN_IN),
        in_specs=[
            pl.BlockSpec((tm, D_MODEL), lambda i, j: (i, 0)),
            pl.BlockSpec((None, 3, D_MODEL), lambda i, j: (mod_row(i), 0, 0)),
            pl.BlockSpec((1, D_MODEL), lambda i, j: (0, 0)),
            pl.BlockSpec((D_MODEL, TN_IN), lambda i, j: (0, j)),
        ],
        out_specs=pl.BlockSpec((tm, TN_IN), lambda i, j: (i, j)),
        out_shape=jax.ShapeDtypeStruct((t, MIX_COLS), F32),
        scratch_shapes=[pltpu.VMEM((tm, D_MODEL), BF16)],
        compiler_params=pltpu.CompilerParams(
            dimension_semantics=("parallel", "arbitrary"), vmem_limit_bytes=VMEM_LIMIT),
        name="in_projection",
    )(x, mods, g, w1)


def _merge_kernel(x_ref, mod_ref, g_ref, oa_ref, ob_ref, oc_ref, od_ref, wz_ref, wg_ref,
                  wb_ref, wo_ref, *rest, final):
    if final:
        fg_ref, o_ref, y_ref = rest
    else:
        (o_ref,) = rest
    x = x_ref[...]
    mod = mod_ref[...]
    h = _modulated_norm(x, g_ref[...], mod).astype(BF16)
    acc = None
    for k, b_ref in enumerate((oa_ref, ob_ref, oc_ref, od_ref)):
        z = _dot(h, wz_ref[k])
        br = (b_ref[...] * _silu(z)).astype(BF16)
        yb = _dot(br, wb_ref[k])
        term = jax.nn.sigmoid(_dot(h, wg_ref[k])) * yb
        acc = term if acc is None else acc + term
    y = _dot(acc.astype(BF16), wo_ref[...])
    out = x + mod[2:3, :] * y
    o_ref[...] = out
    if final:
        y_ref[...] = _rms(out) * fg_ref[...]


def _merge(x, mods, mod_row, g, outs, wz, wg, wb, wo, final_g, tm):
    t = x.shape[0]
    final = final_g is not None
    const = dict(pipeline_mode=pl.Buffered(1))
    in_specs = [
        pl.BlockSpec((tm, D_MODEL), lambda i: (i, 0)),
        pl.BlockSpec((None, 3, D_MODEL), lambda i: (mod_row(i), 0, 0)),
        pl.BlockSpec((1, D_MODEL), lambda i: (0, 0)),
    ] + [pl.BlockSpec((tm, MIX_W), lambda i: (i, 0)) for _ in range(N_BRANCH)] + [
        pl.BlockSpec((N_BRANCH, D_MODEL, MIX_W), lambda i: (0, 0, 0), **const),
        pl.BlockSpec((N_BRANCH, D_MODEL, D_MODEL), lambda i: (0, 0, 0), **const),
        pl.BlockSpec((N_BRANCH, MIX_W, D_MODEL), lambda i: (0, 0, 0), **const),
        pl.BlockSpec((D_MODEL, D_MODEL), lambda i: (0, 0), **const),
    ]
    args = [x, mods, g, *outs, wz, wg, wb, wo]
    out_specs = [pl.BlockSpec((tm, D_MODEL), lambda i: (i, 0))]
    out_shape = [jax.ShapeDtypeStruct((t, D_MODEL), F32)]
    if final:
        in_specs.append(pl.BlockSpec((1, D_MODEL), lambda i: (0, 0)))
        args.append(final_g)
        out_specs.append(pl.BlockSpec((tm, D_MODEL), lambda i: (i, 0)))
        out_shape.append(jax.ShapeDtypeStruct((t, D_MODEL), F32))
    res = pl.pallas_call(
        functools.partial(_merge_kernel, final=final),
        grid=(t // tm,),
        in_specs=in_specs,
        out_specs=out_specs,
        out_shape=out_shape,
        compiler_params=pltpu.CompilerParams(
            dimension_semantics=("parallel",), vmem_limit_bytes=VMEM_LIMIT),
        name="merge",
    )(*args)
    return res if final else (res[0], None)


def _state_specs(layer):
    s_in = pl.BlockSpec((None, None, 2, None, D_HEAD, D_HEAD), lambda b, h: (b, layer, 0, h, 0, 0))
    s_out = pl.BlockSpec((None, 2, None, D_HEAD, D_HEAD), lambda b, h: (b, 0, h, 0, 0))
    return s_in, s_out


def _col_spec(n_tok, cb):
    return pl.BlockSpec((n_tok, LANES), lambda b, h: (b, cb + h))


def _head_mixer_call(kernel, name, bsz, n_tok, in_specs, args, scratch=()):
    _, s_out = _state_specs(0)
    return pl.pallas_call(
        kernel,
        grid=(bsz, N_HEAD),
        in_specs=in_specs,
        out_specs=[pl.BlockSpec((n_tok, LANES), lambda b, h: (b, h)), s_out],
        out_shape=[jax.ShapeDtypeStruct((bsz * n_tok, MIX_W), F32),
                   jax.ShapeDtypeStruct((bsz, 2, N_HEAD, D_HEAD, D_HEAD), F32)],
        scratch_shapes=list(scratch),
        compiler_params=pltpu.CompilerParams(
            dimension_semantics=("parallel", "arbitrary"), vmem_limit_bytes=VMEM_LIMIT),
        name=name,
    )(*args)


def _zero_rows(o_ref, n_tok):
    def body(c, carry):
        o_ref[_rows(c), :] = jnp.zeros((CHUNK, LANES), F32)
        return carry

    lax.fori_loop(0, n_tok // CHUNK, body, 0)


def _norm_rows(o_ref, n_tok, g_ref):
    def body(c, carry):
        o_ref[_rows(c), :] = _rms(o_ref[_rows(c), :]) * g_ref[...]
        return carry

    lax.fori_loop(0, n_tok // CHUNK, body, 0)


def _ret_kernel(lg_ref, q_ref, k_ref, v_ref, *rest, n_tok, rope, has_init):
    rest = list(rest)
    cos_ref, sin_ref = (rest.pop(0), rest.pop(0)) if rope else (None, None)
    s0_ref = rest.pop(0) if has_init else None
    o_ref, sf_ref, kvb_ref = rest
    h = pl.program_id(1)
    size = RET_CHUNK
    nc = n_tok // size
    lg_f, lg_b = lg_ref[0, h], lg_ref[1, h]
    pos_c = lax.broadcasted_iota(jnp.int32, (size, 1), 0).astype(F32)
    pos_r = lax.broadcasted_iota(jnp.int32, (1, size), 1).astype(F32)
    rel = pos_c - pos_r
    lower, upper = rel >= 0, rel <= 0
    dsum = (jnp.where(lower, jnp.exp(jnp.where(lower, rel, 0.0) * lg_f), 0.0)
            + jnp.where(upper, jnp.exp(jnp.where(upper, -rel, 0.0) * lg_b), 0.0))
    q_dec_f = jnp.exp((pos_c + 1.0) * lg_f)
    k_dec_f = jnp.exp((size - 1.0 - pos_c) * lg_f)
    q_dec_b = jnp.exp((size - pos_c) * lg_b)
    k_dec_b = jnp.exp(pos_c * lg_b)
    c_dec_f = jnp.exp(jnp.full((1, D_HEAD), size * lg_f, F32))
    c_dec_b = jnp.exp(jnp.full((1, D_HEAD), size * lg_b, F32))

    def rotate(x, rows):
        if not rope:
            return x
        return x * cos_ref[rows, :] + pltpu.roll(x, D_HEAD // 2, 1) * sin_ref[rows, :]

    unroll = 2 if nc % 2 == 0 else 1

    def forward(i, s):
        cs = [i * unroll + j for j in range(unroll)]
        rows = [_rows(c, size) for c in cs]
        q = [rotate(q_ref[r, :], r) for r in rows]
        k = [rotate(k_ref[r, :], r) * D_HEAD ** -0.5 for r in rows]
        v = [v_ref[r, :] for r in rows]
        scores = [_dot_nt(qj, kj) for qj, kj in zip(q, k)]
        kv_b = [_dot_tn(kj * k_dec_b, vj) for kj, vj in zip(k, v)]
        kv_f = [_dot_tn(kj * k_dec_f, vj) for kj, vj in zip(k, v)]
        inter = []
        for j in range(unroll):
            inter.append(_dot(q[j] * q_dec_f, s))
            s = s * c_dec_f + kv_f[j]
        for j in range(unroll):
            o_ref[rows[j], :] = _dot(scores[j] * dsum, v[j]) + inter[j]
            kvb_ref[cs[j]] = kv_b[j]
        return s

    s0 = s0_ref[0] if has_init else jnp.zeros((D_HEAD, D_HEAD), F32)
    sf_ref[0] = lax.fori_loop(0, nc // unroll, forward, s0)

    def backward(i, s):
        for j in range(unroll):
            c = nc - 1 - (i * unroll + j)
            rows = _rows(c, size)
            q = rotate(q_ref[rows, :], rows)
            o_ref[rows, :] = _rms(o_ref[rows, :] + _dot(q * q_dec_b, s))
            s = s * c_dec_b + kvb_ref[c]
        return s

    s0 = s0_ref[1] if has_init else jnp.zeros((D_HEAD, D_HEAD), F32)
    sf_ref[1] = lax.fori_loop(0, nc // unroll, backward, s0)


def _retention(proj, log_gamma, rope, state, layer, bsz, n_tok):
    in_specs = [pl.BlockSpec(memory_space=pltpu.SMEM),
                _col_spec(n_tok, CB_AQ), _col_spec(n_tok, CB_AK), _col_spec(n_tok, CB_AV)]
    args = [log_gamma, proj, proj, proj]
    if rope is not None:
        in_specs += [pl.BlockSpec((n_tok, LANES), lambda b, h: (0, 0))] * 2
        args += list(rope)
    if state is not None:
        in_specs.append(_state_specs(layer)[0])
        args.append(state)
    kernel = functools.partial(_ret_kernel, n_tok=n_tok, rope=rope is not None,
                               has_init=state is not None)
    scratch = [pltpu.VMEM((n_tok // RET_CHUNK, D_HEAD, D_HEAD), F32)]
    return _head_mixer_call(kernel, "retention", bsz, n_tok, in_specs, args, scratch)


def _tile_scan(a, u, reverse):
    row = lax.broadcasted_iota(jnp.int32, a.shape, 0)
    for s in (1, 2, 4):
        if reverse:
            valid = row < SUBLANES - s
            shift = SUBLANES - s
        else:
            valid = row >= s
            shift = s
        a_sh = jnp.where(valid, pltpu.roll(a, shift, 0), 1.0)
        u_sh = jnp.where(valid, pltpu.roll(u, shift, 0), 0.0)
        u = a * u_sh + u
        a = a * a_sh
    return a, u


def _lru_kernel(x_ref, cw_ref, cb_ref, gw_ref, gb_ref, lam_ref, *rest, n_tok, has_init):
    rest = list(rest)
    h0_ref = rest.pop(0) if has_init else None
    o_ref, hl_ref = rest
    nc = n_tok // CHUNK
    n_tile = CHUNK // SUBLANES
    cw = cw_ref[...]
    cb = cb_ref[...]
    _zero_rows(o_ref, n_tok)

    def body(c, carry):
        ccs = (c, nc - 1 - c)
        xb = [_conv_chunk(x_ref, cc, nc, n_tok, cw) + cb for cc in ccs]
        pre = [(_dot(xb[d], gw_ref[d, 0]), _dot(xb[d], gw_ref[d, 1])) for d in range(2)]
        a, u = [], []
        for d in range(2):
            coef = -LRU_C * jax.nn.softplus(-lam_ref[d])
            r = jax.nn.sigmoid(pre[d][0] + gb_ref[2 * d])
            i = jax.nn.sigmoid(pre[d][1] + gb_ref[2 * d + 1])
            log_a = coef * r
            a_d = jnp.exp(log_a)
            one_minus_a2 = -jnp.tanh(log_a) * (a_d * a_d + 1.0)
            a.append(a_d)
            u.append(jnp.sqrt(jnp.maximum(one_minus_a2, 1e-12)) * (i * xb[d]))
        carry = list(carry)
        tiles = [[None] * n_tile, [None] * n_tile]
        for step in range(n_tile):
            for d in range(2):
                t = n_tile - 1 - step if d == 1 else step
                last = 0 if d == 1 else SUBLANES - 1
                sl = slice(t * SUBLANES, (t + 1) * SUBLANES)
                a_t, u_t = _tile_scan(a[d][sl, :], u[d][sl, :], d == 1)
                h_t = a_t * carry[d] + u_t
                carry[d] = h_t[last:last + 1, :]
                tiles[d][t] = h_t
        for d in range(2):
            rows = _rows(ccs[d])
            o_ref[rows, :] = o_ref[rows, :] + jnp.concatenate(tiles[d], axis=0)
        return tuple(carry)

    zero = jnp.zeros((1, LANES), F32)
    h0 = (h0_ref[0], h0_ref[1]) if has_init else (zero, zero)
    h_f, h_b = lax.fori_loop(0, nc, body, h0)
    hl_ref[0] = h_f
    hl_ref[1] = h_b


def _rglru(proj, conv_w, conv_b, gate_w, gate_b, lam, state, layer, bsz, n_tok):
    n_pair = MIX_W // LANES
    in_specs = [
        pl.BlockSpec((n_tok, LANES), lambda b, p: (b, CB_BX + p)),
        pl.BlockSpec((CONV_K, LANES), lambda b, p: (0, p)),
        pl.BlockSpec((1, LANES), lambda b, p: (0, p)),
        pl.BlockSpec((2, 2, None, LANES, LANES), lambda b, p: (0, 0, p, 0, 0)),
        pl.BlockSpec((4, 1, LANES), lambda b, p: (0, 0, p)),
        pl.BlockSpec((2, 1, LANES), lambda b, p: (0, 0, p)),
    ]
    args = [proj, conv_w, conv_b, gate_w, gate_b, lam]
    if state is not None:
        in_specs.append(pl.BlockSpec((None, None, 2, 1, LANES), lambda b, p: (b, layer, 0, 0, p)))
        args.append(state)
    return pl.pallas_call(
        functools.partial(_lru_kernel, n_tok=n_tok, has_init=state is not None),
        grid=(bsz, n_pair),
        in_specs=in_specs,
        out_specs=[pl.BlockSpec((n_tok, LANES), lambda b, p: (b, p)),
                   pl.BlockSpec((None, 2, 1, LANES), lambda b, p: (b, 0, 0, p))],
        out_shape=[jax.ShapeDtypeStruct((bsz * n_tok, MIX_W), F32),
                   jax.ShapeDtypeStruct((bsz, 2, 1, MIX_W), F32)],
        compiler_params=pltpu.CompilerParams(
            dimension_semantics=("parallel", "arbitrary"), vmem_limit_bytes=VMEM_LIMIT),
        name="rglru",
    )(*args)


def _unit_lower_inverses(mats):
    wide = PACK * CHUNK
    i = lax.broadcasted_iota(jnp.int32, (CHUNK, wide), 0)
    j = lax.broadcasted_iota(jnp.int32, (CHUNK, wide), 1)
    eye = jnp.where(i == j % CHUNK, 1.0, 0.0)
    col_blk = j // CHUNK

    def block_diag(x):
        return jnp.concatenate([jnp.where(col_blk == b, x, jnp.zeros_like(x)) for b in range(PACK)], axis=0)

    steps = CHUNK.bit_length() - 2
    ps = [-jnp.concatenate(mats[g:g + PACK], axis=1) for g in range(0, len(mats), PACK)]
    ts = [eye + p for p in ps]
    halves = [_split2(p) for p in ps]
    diags = [(block_diag(hi), block_diag(lo)) for hi, lo in halves]
    for step in range(steps):
        if step < NEUMANN_SPLIT_STEPS:
            ps = [_dot_split(hp, dg) for hp, dg in zip(halves, diags)]
            halves = [_split2(p) for p in ps]
            diags = [(block_diag(hi), block_diag(lo)) for hi, lo in halves]
            ts = [t + _dot_split(_split2(t), dg) for t, dg in zip(ts, diags)]
        else:
            ps = [_dot(hp[0], dg[0]) for hp, dg in zip(halves, diags)]
            halves = [(p.astype(BF16), None) for p in ps]
            diags = [(block_diag(hp[0]), None) for hp in halves]
            ts = [t + _dot(t.astype(BF16), dg[0]) for t, dg in zip(ts, diags)]
    return [t[:, b * CHUNK:(b + 1) * CHUNK] for t in ts for b in range(PACK)]


def _gdn_kernel(na_ref, dtb_ref, q_ref, k_ref, v_ref, ab_ref, wq_ref, wk_ref, wv_ref, g_ref,
                *rest, n_tok, has_init):
    rest = list(rest)
    s0_ref = rest.pop(0) if has_init else None
    o_ref, sf_ref, qt_ref, kw_ref, bm_ref, egl_ref = rest
    h = pl.program_id(1)
    nc = n_tok // CHUNK
    group = min(GROUP, nc)
    lane = lax.broadcasted_iota(jnp.int32, (CHUNK, LANES), 1)
    masks = {d: (_order_mask(CHUNK, d == 1, strict=False), _order_mask(CHUNK, d == 1, strict=True))
             for d in range(2)}

    def front(i, carry):
        chunks = [i * group + j for j in range(group)]
        rows = [_rows(c) for c in chunks]
        q, k, v, ab = [], [], [], []
        for c, r in zip(chunks, rows):
            qc = _silu(_conv_chunk(q_ref, c, nc, n_tok, wq_ref[...]))
            kc = _silu(_conv_chunk(k_ref, c, nc, n_tok, wk_ref[...]))
            q.append(qc * (lax.rsqrt(jnp.sum(qc * qc, axis=-1, keepdims=True) + EPS) * D_HEAD ** -0.5))
            k.append(kc * lax.rsqrt(jnp.sum(kc * kc, axis=-1, keepdims=True) + EPS))
            v.append(_silu(_conv_chunk(v_ref, c, nc, n_tok, wv_ref[...])))
            ab.append(ab_ref[r, :])
        ctx = [(j, d) for j in range(group) for d in range(2)]
        beta, both = [], []
        for j, d in ctx:
            incl, strict = masks[d]
            a_in = jnp.sum(jnp.where(lane == d * N_HEAD + h, ab[j], 0.0), axis=1, keepdims=True)
            b_in = jnp.sum(jnp.where(lane == (2 + d) * N_HEAD + h, ab[j], 0.0), axis=1, keepdims=True)
            g = na_ref[d, h] * jax.nn.softplus(a_in + dtb_ref[d, h])
            beta.append(jax.nn.sigmoid(b_in))
            rhs = jnp.concatenate([jnp.broadcast_to(g, (CHUNK, CHUNK)), g * strict.astype(F32)], axis=1)
            both.append(_dot_mask(incl.astype(BF16), rhs))
        qk_kk = [_dot_nt(jnp.concatenate([q[j], k[j]], axis=0), k[j]) for j in range(group)]
        gc, gam, a_mat = [], [], []
        for m, (j, d) in enumerate(ctx):
            incl, strict = masks[d]
            gc.append(jnp.broadcast_to(both[m][:, 0:1], (CHUNK, LANES)))
            gam.append(jnp.where(incl, jnp.exp(jnp.where(incl, both[m][:, CHUNK:], 0.0)), 0.0))
            a_mat.append(jnp.where(strict, beta[m] * qk_kk[j][CHUNK:] * gam[m], 0.0))
        t_inv = _unit_lower_inverses(a_mat)
        e_gc = [jnp.exp(x) for x in gc]
        wu = [_dot(t_inv[m], jnp.concatenate([k[j] * (beta[m] * e_gc[m]), v[j] * beta[m]], axis=1))
              for m, (j, d) in enumerate(ctx)]
        a_wu = [_dot(qk_kk[j][:CHUNK] * gam[m], wu[m]) for m, (j, d) in enumerate(ctx)]
        gl = [gc[m][(0 if d == 1 else CHUNK - 1):(1 if d == 1 else CHUNK), :] for m, (j, d) in enumerate(ctx)]
        kd_wu = [_dot_tn(k[j] * jnp.exp(gl[m] - gc[m]), wu[m]) for m, (j, d) in enumerate(ctx)]
        for m, (j, d) in enumerate(ctx):
            qt_ref[d, rows[j], :] = (q[j] * e_gc[m] - a_wu[m][:, :D_HEAD]).astype(BF16)
            kw_ref[d, chunks[j]] = kd_wu[m][:, :D_HEAD].astype(BF16)
            bm_ref[d, chunks[j]] = kd_wu[m][:, D_HEAD:]
            egl_ref[d, chunks[j]] = jnp.broadcast_to(jnp.exp(gl[m]), (SUBLANES, LANES))
        for j in range(group):
            o_ref[rows[j], :] = a_wu[2 * j][:, D_HEAD:] + a_wu[2 * j + 1][:, D_HEAD:]
        return carry

    lax.fori_loop(0, nc // group, front, 0)

    unroll = 2

    def back(i, carry):
        s = list(carry)
        for u in range(unroll):
            step = i * unroll + u
            for d in range(2):
                c = (nc - 1 - step) if d == 1 else step
                rows = _rows(c)
                s_bf = s[d].astype(BF16)
                o_ref[rows, :] = o_ref[rows, :] + _dot(qt_ref[d, rows, :], s_bf)
                s[d] = s[d] * egl_ref[d, c][0:1, :] - _dot(kw_ref[d, c], s_bf) + bm_ref[d, c]
        return tuple(s)

    zero = jnp.zeros((D_HEAD, D_HEAD), F32)
    s0 = (s0_ref[0], s0_ref[1]) if has_init else (zero, zero)
    s_f, s_b = lax.fori_loop(0, nc // unroll, back, s0)
    sf_ref[0] = s_f
    sf_ref[1] = s_b
    _norm_rows(o_ref, n_tok, g_ref)


def _gated_delta(proj, neg_a, dt_bias, conv_w, norm_g, state, layer, bsz, n_tok):
    smem = pl.BlockSpec(memory_space=pltpu.SMEM)
    in_specs = [smem, smem,
                _col_spec(n_tok, CB_CQ), _col_spec(n_tok, CB_CK), _col_spec(n_tok, CB_CV),
                pl.BlockSpec((n_tok, LANES), lambda b, h: (b, CB_CAB)),
                pl.BlockSpec((CONV_K, LANES), lambda b, h: (0, h)),
                pl.BlockSpec((CONV_K, LANES), lambda b, h: (0, N_HEAD + h)),
                pl.BlockSpec((CONV_K, LANES), lambda b, h: (0, 2 * N_HEAD + h)),
                pl.BlockSpec((1, D_HEAD), lambda b, h: (0, 0))]
    args = [neg_a, dt_bias, proj, proj, proj, proj, conv_w, conv_w, conv_w, norm_g]
    if state is not None:
        in_specs.append(_state_specs(layer)[0])
        args.append(state)
    kernel = functools.partial(_gdn_kernel, n_tok=n_tok, has_init=state is not None)
    nc = n_tok // CHUNK
    scratch = [pltpu.VMEM((2, n_tok, D_HEAD), BF16),
               pltpu.VMEM((2, nc, D_HEAD, D_HEAD), BF16),
               pltpu.VMEM((2, nc, D_HEAD, D_HEAD), F32),
               pltpu.VMEM((2, nc, SUBLANES, LANES), F32)]
    return _head_mixer_call(kernel, "gated_delta", bsz, n_tok, in_specs, args, scratch)


def _block_rows(x, row):
    nb = CHUNK // SUB
    x3 = x.reshape(nb, SUB, LANES)[:, row:row + 1, :]
    return jnp.broadcast_to(x3, (nb, SUB, LANES)).reshape(CHUNK, LANES)


def _hgrn_kernel(q_ref, f0_ref, f1_ref, v_ref, lb_ref, g_ref, *rest, n_tok, has_init):
    rest = list(rest)
    s0_ref = rest.pop(0) if has_init else None
    o_ref, sf_ref = rest
    nc = n_tok // CHUNK
    nb = CHUNK // SUB
    blk_c = lax.broadcasted_iota(jnp.int32, (CHUNK, 1), 0) // SUB
    blk_i = lax.broadcasted_iota(jnp.int32, (CHUNK, CHUNK), 0) // SUB
    blk_j = lax.broadcasted_iota(jnp.int32, (CHUNK, CHUNK), 1) // SUB
    _zero_rows(o_ref, n_tok)

    def body(i, carry):
        ctx = [(d, (nc - 1 - (i * GROUP_D + j)) if d == 1 else i * GROUP_D + j)
               for j in range(GROUP_D) for d in range(2)]
        n = len(ctx)
        incl = {d: _order_mask(CHUNK, d == 1, strict=False) for d in range(2)}
        rows = [_rows(cc) for _, cc in ctx]
        q = [_silu(q_ref[r, :]) * D_HEAD ** -0.5 for r in rows]
        v = [v_ref[r, :] for r in rows]
        k, log_f, bc = [], [], []
        for (d, _), r in zip(ctx, rows):
            zf = (f1_ref if d == 1 else f0_ref)[r, :]
            lb = lb_ref[d:d + 1, :]
            k.append((1.0 - lb) * jax.nn.sigmoid(-zf))
            f_gate = lb + (1.0 - lb) * jax.nn.sigmoid(zf)
            log_f.append(jnp.log(jnp.maximum(f_gate, 1e-30)))
            bc.append(_dot_mask(incl[d].astype(BF16), log_f[-1]))
        ex = [b - lf for b, lf in zip(bc, log_f)]
        a_diag = []
        for m, (d, _) in enumerate(ctx):
            m_blk = _block_rows(bc[m], SUB // 2 if d == 1 else SUB // 2 - 1)
            a_diag.append(_dot_nt(q[m] * jnp.exp(bc[m] - m_blk), k[m] * jnp.exp(m_blk - bc[m])))
        a_mat = []
        for m, (d, _) in enumerate(ctx):
            first = SUB - 1 if d == 1 else 0
            q_c = q[m] * jnp.exp(bc[m] - _block_rows(ex[m], first))
            cross = []
            for blk in range(nb):
                if blk == (nb - 1 if d == 1 else 0):
                    cross.append(jnp.zeros((SUB, CHUNK), F32))
                    continue
                earlier = (blk_c > blk) if d == 1 else (blk_c < blk)
                r_i = ex[m][blk * SUB + first:blk * SUB + first + 1, :]
                k_t = jnp.where(earlier, k[m] * jnp.exp(jnp.where(earlier, r_i - bc[m], 0.0)), 0.0)
                cross.append(_dot_nt(q_c[blk * SUB:(blk + 1) * SUB, :], k_t))
            diag = incl[d] & (blk_i == blk_j)
            a_mat.append(jnp.where(diag, a_diag[m], 0.0) + jnp.concatenate(cross, axis=0))
        b_last = [bc[m][(0 if d == 1 else CHUNK - 1):(1 if d == 1 else CHUNK), :]
                  for m, (d, _) in enumerate(ctx)]
        kv = [_dot_tn(v[m], k[m] * jnp.exp(b_last[m] - bc[m])) for m in range(n)]
        intra = [_dot(a_mat[m], v[m]) for m in range(n)]
        st = list(carry)
        for m, (d, _) in enumerate(ctx):
            o = intra[m] + _dot_nt(q[m] * jnp.exp(bc[m]), st[d])
            o_ref[rows[m], :] = o_ref[rows[m], :] + o
            st[d] = st[d] * jnp.exp(b_last[m]) + kv[m]
        return tuple(st)

    zero = jnp.zeros((D_HEAD, D_HEAD), F32)
    st0 = (s0_ref[0].T, s0_ref[1].T) if has_init else (zero, zero)
    st_f, st_b = lax.fori_loop(0, nc // GROUP_D, body, st0)
    sf_ref[0] = st_f.T
    sf_ref[1] = st_b.T
    _norm_rows(o_ref, n_tok, g_ref)


def _hgrn2(proj, lower_bound, norm_g, state, layer, bsz, n_tok):
    in_specs = [_col_spec(n_tok, CB_DQ), _col_spec(n_tok, CB_DF), _col_spec(n_tok, CB_DF + N_HEAD),
                _col_spec(n_tok, CB_DI),
                pl.BlockSpec((2, D_HEAD), lambda b, h: (0, h)),
                pl.BlockSpec((1, D_HEAD), lambda b, h: (0, 0))]
    args = [proj, proj, proj, proj, lower_bound, norm_g]
    if state is not None:
        in_specs.append(_state_specs(layer)[0])
        args.append(state)
    kernel = functools.partial(_hgrn_kernel, n_tok=n_tok, has_init=state is not None)
    return _head_mixer_call(kernel, "hgrn2", bsz, n_tok, in_specs, args)


def _rope_tables(n_tok):
    n_freq = D_HEAD // 4
    inv = ROPE_BASE ** (-jnp.arange(n_freq, dtype=F32) / n_freq)
    rows = n_tok // GRID_W
    r = jnp.repeat(jnp.arange(rows, dtype=F32), GRID_W)
    col = jnp.tile(jnp.arange(GRID_W, dtype=F32), rows)
    ang = jnp.concatenate([r[:, None] * inv, col[:, None] * inv], axis=-1)
    cos, sin = jnp.cos(ang), jnp.sin(ang)
    return jnp.concatenate([cos, cos], axis=-1), jnp.concatenate([-sin, sin], axis=-1)


def _split_w_in(w):
    m = MIX_W
    a = w[:, 0:4 * m]
    b = w[:, 4 * m:6 * m]
    c = w[:, 6 * m:10 * m + 16]
    dd = w[:, 10 * m + 16:15 * m + 16]
    mg = w[:, 15 * m + 16:]
    cab = jnp.pad(c[:, 4 * m:], ((0, 0), (0, LANES - 16)))
    w1 = jnp.concatenate([a[:, :3 * m], b[:, :m], c[:, :3 * m], dd[:, :4 * m], cab], axis=1)
    wz = jnp.stack([a[:, 3 * m:], b[:, m:], c[:, 3 * m:4 * m], dd[:, 4 * m:]], axis=0)
    wg = jnp.moveaxis(mg.reshape(D_MODEL, N_BRANCH, D_MODEL), 1, 0)
    return w1.astype(BF16), wz.astype(BF16), wg.astype(BF16)


def _pair_block_diag(gw):
    z = jnp.zeros_like(gw[:, :, 0::2])
    top = jnp.concatenate([gw[:, :, 0::2], z], axis=-1)
    bot = jnp.concatenate([z, gw[:, :, 1::2]], axis=-1)
    return jnp.concatenate([top, bot], axis=-2)


def kernel(x_prompt, x_sample, state_ret, state_lru, state_gdn, state_hgrn, c, c_ctx, norm_g,
           w_mod, b_mod, w_in, ret_decay, lru_conv_w, lru_conv_b, lru_gate_w, lru_gate_b,
           lru_lambda, gdn_conv_w, gdn_a_log, gdn_dt_bias, gdn_norm_g, hgrn_lb, hgrn_norm_g,
           w_branch, w_out, final_norm_g):
    bp, n_ctx, _ = x_prompt.shape
    bs, n_lat, _ = x_sample.shape

    lb_sm = jax.nn.softmax(hgrn_lb.astype(F32), axis=0)
    lower_bounds = jnp.cumsum(lb_sm, axis=0) - lb_sm[0]
    log_gamma = jnp.log1p(-jnp.exp(ret_decay.astype(F32)))
    gdn_neg_a = -jnp.exp(gdn_a_log.astype(F32))
    rope = _rope_tables(n_lat)
    state_lru5 = state_lru.reshape(bs, DEPTH, 2, 1, MIX_W)
    layer_w = [(*_split_w_in(w_in[l]), w_branch[l].astype(BF16), w_out[l].astype(BF16),
                _pair_block_diag(lru_gate_w[l])) for l in range(DEPTH)]

    mvecs = jnp.concatenate([c_ctx[None, :], c, jnp.zeros((SUBLANES - 1 - bs, D_MODEL), F32)], axis=0)
    mods = _mod_vectors(mvecs, w_mod, b_mod).reshape(DEPTH, SUBLANES, 3, D_MODEL)

    tm, tm_in = 512, 1024
    groups = [
        dict(x=x_prompt.reshape(bp * n_ctx, D_MODEL), bsz=bp, n_tok=n_ctx, rope=None, cached=False,
             mod_row=lambda i, t: 0),
        dict(x=x_sample.reshape(bs * n_lat, D_MODEL), bsz=bs, n_tok=n_lat, rope=rope, cached=True,
             mod_row=lambda i, t: 1 + (i * t) // n_lat),
    ]
    new_states = [[], [], [], []]
    finals = []
    for grp in groups:
        x, bsz, n_tok = grp["x"], grp["bsz"], grp["n_tok"]
        cached = grp["cached"]
        for l in range(DEPTH):
            w1, wz, wg, wb, wo, gate_w = layer_w[l]
            g = norm_g[l][None, :]
            proj = _in_projection(x, mods[l], functools.partial(grp["mod_row"], t=tm_in), g, w1, tm_in)
            o_a, s_a = _retention(proj, log_gamma[l], grp["rope"], state_ret if cached else None,
                                  l, bsz, n_tok)
            gate_b = lru_gate_b[l].reshape(4, 1, MIX_W)
            o_b, s_b = _rglru(proj, lru_conv_w[l], lru_conv_b[l][None, :], gate_w, gate_b,
                              lru_lambda[l][:, None, :], state_lru5 if cached else None, l, bsz, n_tok)
            o_c, s_c = _gated_delta(proj, gdn_neg_a[l], gdn_dt_bias[l].astype(F32), gdn_conv_w[l],
                                    gdn_norm_g[l][None, :], state_gdn if cached else None, l, bsz, n_tok)
            o_d, s_d = _hgrn2(proj, lower_bounds[l], hgrn_norm_g[l][None, :],
                              state_hgrn if cached else None, l, bsz, n_tok)
            final_g = final_norm_g[None, :] if l == DEPTH - 1 else None
            x, y = _merge(x, mods[l], functools.partial(grp["mod_row"], t=tm), g, (o_a, o_b, o_c, o_d),
                          wz, wg, wb, wo, final_g, tm)
            if not cached:
                for acc, s in zip(new_states, (s_a, s_b, s_c, s_d)):
                    acc.append(s)
        finals.append(y)

    y_prompt = finals[0].reshape(bp, n_ctx, D_MODEL)
    y_sample = finals[1].reshape(bs, n_lat, D_MODEL)
    new_ret = jnp.stack(new_states[0], axis=1)
    new_lru = jnp.stack(new_states[1], axis=1).reshape(bp, DEPTH, 2, MIX_W)
    new_gdn = jnp.stack(new_states[2], axis=1)
    new_hgrn = jnp.stack(new_states[3], axis=1)
    return (y_prompt, y_sample, new_ret, new_lru, new_gdn, new_hgrn)
```

```python
import functools

import jax
import jax.numpy as jnp
from jax import lax
from jax.experimental import pallas as pl
from jax.experimental.pallas import tpu as pltpu

F32 = jnp.float32
BF16 = jnp.bfloat16

D_MODEL = 1024
DEPTH = 4
MIX_W = 512
N_BRANCH = 4
N_HEAD = 4
D_HEAD = 128
GRID_W = 64
LRU_C = 8.0
CONV_K = 4
CONV_LEFT = 2
CHUNK = 64
RET_CHUNK = 256
SUB = 16
GROUP = 16
GROUP_D = 4
LRU_UNROLL = 4
PACK = 4
NEUMANN_SPLIT_STEPS = 3
ROPE_BASE = 10000.0
EPS = 1e-6
LANES = 128
SUBLANES = 8
VMEM_LIMIT = 56 * 1024 * 1024

CB_AQ, CB_AK, CB_AV, CB_BX, CB_CQ, CB_CK, CB_CV, CB_DQ, CB_DF, CB_DI, CB_CAB = (
    0, 4, 8, 12, 16, 20, 24, 28, 32, 40, 44)
MIX_COLS = 45 * LANES
TN_IN = 1920


def _dot(a, b):
    return jnp.dot(a, b, preferred_element_type=F32)


def _dot_nt(a, b):
    return lax.dot_general(a, b, (((1,), (1,)), ((), ())), preferred_element_type=F32)


def _dot_tn(a, b):
    return lax.dot_general(a, b, (((0,), (0,)), ((), ())), preferred_element_type=F32)


def _split2(x):
    hi = x.astype(BF16)
    lo = (x - hi.astype(F32)).astype(BF16)
    return hi, lo


def _dot_split(a, b):
    return _dot(a[0], b[0]) + (_dot(a[0], b[1]) + _dot(a[1], b[0]))


def _dot_mask(mask_bf16, x):
    hi = x.astype(BF16)
    r1 = x - hi.astype(F32)
    mid = r1.astype(BF16)
    lo = (r1 - mid.astype(F32)).astype(BF16)
    return _dot(mask_bf16, hi) + (_dot(mask_bf16, mid) + _dot(mask_bf16, lo))


def _order_mask(n, reverse, strict):
    i = lax.broadcasted_iota(jnp.int32, (n, n), 0)
    j = lax.broadcasted_iota(jnp.int32, (n, n), 1)
    if reverse:
        return (j > i) if strict else (j >= i)
    return (j < i) if strict else (j <= i)


def _rows(cc, size=CHUNK):
    return pl.ds(pl.multiple_of(cc * size, size), size)


def _conv_chunk(x_ref, cc, nc, n_rows, w):
    r0 = pl.multiple_of(cc * CHUNK, CHUNK)
    cur = x_ref[pl.ds(r0, CHUNK), :]
    p0 = pl.multiple_of(jnp.maximum(r0 - SUBLANES, 0), SUBLANES)
    n0 = pl.multiple_of(jnp.minimum(r0 + CHUNK, n_rows - SUBLANES), SUBLANES)
    in_seq = cc % nc
    prev = x_ref[pl.ds(p0, SUBLANES), :] * jnp.where(in_seq > 0, 1.0, 0.0)
    nxt = x_ref[pl.ds(n0, SUBLANES), :] * jnp.where(in_seq < nc - 1, 1.0, 0.0)
    xw = jnp.concatenate([prev, cur, nxt], axis=0)
    rows = CHUNK + 2 * SUBLANES
    acc = None
    for j in range(CONV_K):
        shift = (CONV_LEFT - j) % rows
        rolled = xw if shift == 0 else pltpu.roll(xw, shift, 0)
        term = rolled[SUBLANES:SUBLANES + CHUNK, :] * w[j:j + 1, :]
        acc = term if acc is None else acc + term
    return acc


def _silu(x):
    return x * jax.nn.sigmoid(x)


def _rms(x):
    return x * lax.rsqrt(jnp.mean(x * x, axis=-1, keepdims=True) + EPS)


def _mod_kernel(m_ref, w_ref, b_ref, o_ref):
    m = _silu(m_ref[...])
    o_ref[...] = _dot(m.astype(BF16), w_ref[...].astype(BF16)) + b_ref[...]


def _mod_vectors(mvecs, w_mod, b_mod):
    tn = D_MODEL
    return pl.pallas_call(
        _mod_kernel,
        grid=(DEPTH, 3 * D_MODEL // tn),
        in_specs=[
            pl.BlockSpec((SUBLANES, D_MODEL), lambda l, j: (0, 0)),
            pl.BlockSpec((None, D_MODEL, tn), lambda l, j: (l, 0, j)),
            pl.BlockSpec((None, 1, tn), lambda l, j: (l, 0, j)),
        ],
        out_specs=pl.BlockSpec((None, SUBLANES, tn), lambda l, j: (l, 0, j)),
        out_shape=jax.ShapeDtypeStruct((DEPTH, SUBLANES, 3 * D_MODEL), F32),
        compiler_params=pltpu.CompilerParams(vmem_limit_bytes=VMEM_LIMIT),
        name="mod_vectors",
    )(mvecs, w_mod, b_mod.reshape(DEPTH, 1, 3 * D_MODEL))


def _modulated_norm(x, g, mod):
    return _rms(x) * g * (1.0 + mod[1:2, :]) + mod[0:1, :]


def _inproj_kernel(x_ref, mod_ref, g_ref, w_ref, o_ref, h_ref):
    @pl.when(pl.program_id(1) == 0)
    def _():
        h_ref[...] = _modulated_norm(x_ref[...], g_ref[...], mod_ref[...]).astype(BF16)

    o_ref[...] = _dot(h_ref[...], w_ref[...])


def _in_projection(x, mods, mod_row, g, w1, tm):
    t = x.shape[0]
    return pl.pallas_call(
        _inproj_kernel,
        grid=(t // tm, MIX_COLS // TN_IN),
        in_specs=[
            pl.BlockSpec((tm, D_MODEL), lambda i, j: (i, 0)),
            pl.BlockSpec((None, 3, D_MODEL), lambda i, j: (mod_row(i), 0, 0)),
            pl.BlockSpec((1, D_MODEL), lambda i, j: (0, 0)),
            pl.BlockSpec((D_MODEL, TN_IN), lambda i, j: (0, j)),
        ],
        out_specs=pl.BlockSpec((tm, TN_IN), lambda i, j: (i, j)),
        out_shape=jax.ShapeDtypeStruct((t, MIX_COLS), F32),
        scratch_shapes=[pltpu.VMEM((tm, D_MODEL), BF16)],
        compiler_params=pltpu.CompilerParams(
            dimension_semantics=("parallel", "arbitrary"), vmem_limit_bytes=VMEM_LIMIT),
        name="in_projection",
    )(x, mods, g, w1)


def _merge_kernel(x_ref, mod_ref, g_ref, oa_ref, ob_ref, oc_ref, od_ref, wz_ref, wg_ref,
                  wb_ref, wo_ref, *rest, final):
    if final:
        fg_ref, o_ref, y_ref = rest
    else:
        (o_ref,) = rest
    x = x_ref[...]
    mod = mod_ref[...]
    h = _modulated_norm(x, g_ref[...], mod).astype(BF16)
    acc = None
    for k, b_ref in enumerate((oa_ref, ob_ref, oc_ref, od_ref)):
        z = _dot(h, wz_ref[k])
        br = (b_ref[...] * _silu(z)).astype(BF16)
        yb = _dot(br, wb_ref[k])
        term = jax.nn.sigmoid(_dot(h, wg_ref[k])) * yb
        acc = term if acc is None else acc + term
    y = _dot(acc.astype(BF16), wo_ref[...])
    out = x + mod[2:3, :] * y
    o_ref[...] = out
    if final:
        y_ref[...] = _rms(out) * fg_ref[...]


def _merge(x, mods, mod_row, g, outs, wz, wg, wb, wo, final_g, tm):
    t = x.shape[0]
    final = final_g is not None
    const = dict(pipeline_mode=pl.Buffered(1))
    in_specs = [
        pl.BlockSpec((tm, D_MODEL), lambda i: (i, 0)),
        pl.BlockSpec((None, 3, D_MODEL), lambda i: (mod_row(i), 0, 0)),
        pl.BlockSpec((1, D_MODEL), lambda i: (0, 0)),
    ] + [pl.BlockSpec((tm, MIX_W), lambda i: (i, 0)) for _ in range(N_BRANCH)] + [
        pl.BlockSpec((N_BRANCH, D_MODEL, MIX_W), lambda i: (0, 0, 0), **const),
        pl.BlockSpec((N_BRANCH, D_MODEL, D_MODEL), lambda i: (0, 0, 0), **const),
        pl.BlockSpec((N_BRANCH, MIX_W, D_MODEL), lambda i: (0, 0, 0), **const),
        pl.BlockSpec((D_MODEL, D_MODEL), lambda i: (0, 0), **const),
    ]
    args = [x, mods, g, *outs, wz, wg, wb, wo]
    out_specs = [pl.BlockSpec((tm, D_MODEL), lambda i: (i, 0))]
    out_shape = [jax.ShapeDtypeStruct((t, D_MODEL), F32)]
    if final:
        in_specs.append(pl.BlockSpec((1, D_MODEL), lambda i: (0, 0)))
        args.append(final_g)
        out_specs.append(pl.BlockSpec((tm, D_MODEL), lambda i: (i, 0)))
        out_shape.append(jax.ShapeDtypeStruct((t, D_MODEL), F32))
    res = pl.pallas_call(
        functools.partial(_merge_kernel, final=final),
        grid=(t // tm,),
        in_specs=in_specs,
        out_specs=out_specs,
        out_shape=out_shape,
        compiler_params=pltpu.CompilerParams(
            dimension_semantics=("parallel",), vmem_limit_bytes=VMEM_LIMIT),
        name="merge",
    )(*args)
    return res if final else (res[0], None)


def _state_specs(layer):
    s_in = pl.BlockSpec((None, None, 2, None, D_HEAD, D_HEAD), lambda b, h: (b, layer, 0, h, 0, 0))
    s_out = pl.BlockSpec((None, 2, None, D_HEAD, D_HEAD), lambda b, h: (b, 0, h, 0, 0))
    return s_in, s_out


def _col_spec(n_tok, cb):
    return pl.BlockSpec((n_tok, LANES), lambda b, h: (b, cb + h))


def _head_mixer_call(kernel, name, bsz, n_tok, in_specs, args, scratch=(), seqs=None):
    _, s_out = _state_specs(0)
    if seqs is not None:
        s_out = pl.BlockSpec((seqs, 2, None, D_HEAD, D_HEAD), lambda b, h: (b, 0, h, 0, 0))
    per_step = seqs or 1
    return pl.pallas_call(
        kernel,
        grid=(bsz // per_step, N_HEAD),
        in_specs=in_specs,
        out_specs=[pl.BlockSpec((per_step * n_tok, LANES), lambda b, h: (b, h)), s_out],
        out_shape=[jax.ShapeDtypeStruct((bsz * n_tok, MIX_W), F32),
                   jax.ShapeDtypeStruct((bsz, 2, N_HEAD, D_HEAD, D_HEAD), F32)],
        scratch_shapes=list(scratch),
        compiler_params=pltpu.CompilerParams(
            dimension_semantics=("parallel", "arbitrary"), vmem_limit_bytes=VMEM_LIMIT),
        name=name,
    )(*args)


def _zero_rows(o_ref, n_tok):
    def body(c, carry):
        o_ref[_rows(c), :] = jnp.zeros((CHUNK, LANES), F32)
        return carry

    lax.fori_loop(0, n_tok // CHUNK, body, 0)


def _norm_rows(o_ref, n_tok, g_ref):
    def body(c, carry):
        o_ref[_rows(c), :] = _rms(o_ref[_rows(c), :]) * g_ref[...]
        return carry

    lax.fori_loop(0, n_tok // CHUNK, body, 0)


def _ret_kernel(lg_ref, q_ref, k_ref, v_ref, *rest, n_tok, rope, has_init):
    rest = list(rest)
    cos_ref, sin_ref = (rest.pop(0), rest.pop(0)) if rope else (None, None)
    s0_ref = rest.pop(0) if has_init else None
    o_ref, sf_ref, kvb_ref = rest
    h = pl.program_id(1)
    size = RET_CHUNK
    nc = n_tok // size
    lg_f, lg_b = lg_ref[0, h], lg_ref[1, h]
    pos_c = lax.broadcasted_iota(jnp.int32, (size, 1), 0).astype(F32)
    pos_r = lax.broadcasted_iota(jnp.int32, (1, size), 1).astype(F32)
    rel = pos_c - pos_r
    lower, upper = rel >= 0, rel <= 0
    dsum = (jnp.where(lower, jnp.exp(jnp.where(lower, rel, 0.0) * lg_f), 0.0)
            + jnp.where(upper, jnp.exp(jnp.where(upper, -rel, 0.0) * lg_b), 0.0))
    q_dec_f = jnp.exp((pos_c + 1.0) * lg_f)
    k_dec_f = jnp.exp((size - 1.0 - pos_c) * lg_f)
    q_dec_b = jnp.exp((size - pos_c) * lg_b)
    k_dec_b = jnp.exp(pos_c * lg_b)
    c_dec_f = jnp.exp(jnp.full((1, D_HEAD), size * lg_f, F32))
    c_dec_b = jnp.exp(jnp.full((1, D_HEAD), size * lg_b, F32))

    def rotate(x, rows):
        if not rope:
            return x
        return x * cos_ref[rows, :] + pltpu.roll(x, D_HEAD // 2, 1) * sin_ref[rows, :]

    unroll = 2 if nc % 2 == 0 else 1

    def forward(i, s):
        cs = [i * unroll + j for j in range(unroll)]
        rows = [_rows(c, size) for c in cs]
        q = [rotate(q_ref[r, :], r) for r in rows]
        k = [rotate(k_ref[r, :], r) * D_HEAD ** -0.5 for r in rows]
        v = [v_ref[r, :] for r in rows]
        scores = [_dot_nt(qj, kj) for qj, kj in zip(q, k)]
        kv_b = [_dot_tn(kj * k_dec_b, vj) for kj, vj in zip(k, v)]
        kv_f = [_dot_tn(kj * k_dec_f, vj) for kj, vj in zip(k, v)]
        inter = []
        for j in range(unroll):
            inter.append(_dot(q[j] * q_dec_f, s))
            s = s * c_dec_f + kv_f[j]
        for j in range(unroll):
            o_ref[rows[j], :] = _dot(scores[j] * dsum, v[j]) + inter[j]
            kvb_ref[cs[j]] = kv_b[j]
        return s

    s0 = s0_ref[0] if has_init else jnp.zeros((D_HEAD, D_HEAD), F32)
    sf_ref[0] = lax.fori_loop(0, nc // unroll, forward, s0)

    def backward(i, s):
        for j in range(unroll):
            c = nc - 1 - (i * unroll + j)
            rows = _rows(c, size)
            q = rotate(q_ref[rows, :], rows)
            o_ref[rows, :] = _rms(o_ref[rows, :] + _dot(q * q_dec_b, s))
            s = s * c_dec_b + kvb_ref[c]
        return s

    s0 = s0_ref[1] if has_init else jnp.zeros((D_HEAD, D_HEAD), F32)
    sf_ref[1] = lax.fori_loop(0, nc // unroll, backward, s0)


def _retention(proj, log_gamma, rope, state, layer, bsz, n_tok):
    in_specs = [pl.BlockSpec(memory_space=pltpu.SMEM),
                _col_spec(n_tok, CB_AQ), _col_spec(n_tok, CB_AK), _col_spec(n_tok, CB_AV)]
    args = [log_gamma, proj, proj, proj]
    if rope is not None:
        in_specs += [pl.BlockSpec((n_tok, LANES), lambda b, h: (0, 0))] * 2
        args += list(rope)
    if state is not None:
        in_specs.append(_state_specs(layer)[0])
        args.append(state)
    kernel = functools.partial(_ret_kernel, n_tok=n_tok, rope=rope is not None,
                               has_init=state is not None)
    scratch = [pltpu.VMEM((n_tok // RET_CHUNK, D_HEAD, D_HEAD), F32)]
    return _head_mixer_call(kernel, "retention", bsz, n_tok, in_specs, args, scratch)


def _tile_scan(a, u, reverse):
    row = lax.broadcasted_iota(jnp.int32, a.shape, 0)
    for s in (1, 2, 4):
        if reverse:
            valid = row < SUBLANES - s
            shift = SUBLANES - s
        else:
            valid = row >= s
            shift = s
        a_sh = jnp.where(valid, pltpu.roll(a, shift, 0), 1.0)
        u_sh = jnp.where(valid, pltpu.roll(u, shift, 0), 0.0)
        u = a * u_sh + u
        a = a * a_sh
    return a, u


def _lru_kernel(x_ref, cw_ref, cb_ref, gw_ref, gb_ref, lam_ref, *rest, n_tok, has_init):
    rest = list(rest)
    h0_ref = rest.pop(0) if has_init else None
    o_ref, hl_ref = rest
    nc = n_tok // CHUNK
    n_tile = CHUNK // SUBLANES
    cw = cw_ref[...]
    cb = cb_ref[...]
    _zero_rows(o_ref, n_tok)

    unroll = min(LRU_UNROLL, nc)
    coef = [-LRU_C * jax.nn.softplus(-lam_ref[d]) for d in range(2)]

    def body(i, carry):
        ctx = [(d, (nc - 1 - (i * unroll + j)) if d == 1 else i * unroll + j)
               for j in range(unroll) for d in range(2)]
        xb = [_conv_chunk(x_ref, cc, nc, n_tok, cw) + cb for _, cc in ctx]
        pre = [(_dot(x, gw_ref[d, 0]), _dot(x, gw_ref[d, 1])) for x, (d, _) in zip(xb, ctx)]
        carry = list(carry)
        for m, (d, cc) in enumerate(ctx):
            r = jax.nn.sigmoid(pre[m][0] + gb_ref[2 * d])
            gate = jax.nn.sigmoid(pre[m][1] + gb_ref[2 * d + 1])
            log_a = coef[d] * r
            a = jnp.exp(log_a)
            one_minus_a2 = -jnp.tanh(log_a) * (a * a + 1.0)
            u = jnp.sqrt(jnp.maximum(one_minus_a2, 1e-12)) * (gate * xb[m])
            scans = [_tile_scan(a[t * SUBLANES:(t + 1) * SUBLANES, :], u[t * SUBLANES:(t + 1) * SUBLANES, :],
                                d == 1) for t in range(n_tile)]
            tiles = [None] * n_tile
            last = 0 if d == 1 else SUBLANES - 1
            for t in (range(n_tile - 1, -1, -1) if d == 1 else range(n_tile)):
                a_t, u_t = scans[t]
                tiles[t] = a_t * carry[d] + u_t
                carry[d] = a_t[last:last + 1, :] * carry[d] + u_t[last:last + 1, :]
            rows = _rows(cc)
            o_ref[rows, :] = o_ref[rows, :] + jnp.concatenate(tiles, axis=0)
        return tuple(carry)

    zero = jnp.zeros((1, LANES), F32)
    h0 = (h0_ref[0], h0_ref[1]) if has_init else (zero, zero)
    h_f, h_b = lax.fori_loop(0, nc // unroll, body, h0)
    hl_ref[0] = h_f
    hl_ref[1] = h_b


def _rglru(proj, conv_w, conv_b, gate_w, gate_b, lam, state, layer, bsz, n_tok):
    n_pair = MIX_W // LANES
    in_specs = [
        pl.BlockSpec((n_tok, LANES), lambda b, p: (b, CB_BX + p)),
        pl.BlockSpec((CONV_K, LANES), lambda b, p: (0, p)),
        pl.BlockSpec((1, LANES), lambda b, p: (0, p)),
        pl.BlockSpec((2, 2, None, LANES, LANES), lambda b, p: (0, 0, p, 0, 0)),
        pl.BlockSpec((4, 1, LANES), lambda b, p: (0, 0, p)),
        pl.BlockSpec((2, 1, LANES), lambda b, p: (0, 0, p)),
    ]
    args = [proj, conv_w, conv_b, gate_w, gate_b, lam]
    if state is not None:
        in_specs.append(pl.BlockSpec((None, None, 2, 1, LANES), lambda b, p: (b, layer, 0, 0, p)))
        args.append(state)
    return pl.pallas_call(
        functools.partial(_lru_kernel, n_tok=n_tok, has_init=state is not None),
        grid=(bsz, n_pair),
        in_specs=in_specs,
        out_specs=[pl.BlockSpec((n_tok, LANES), lambda b, p: (b, p)),
                   pl.BlockSpec((None, 2, 1, LANES), lambda b, p: (b, 0, 0, p))],
        out_shape=[jax.ShapeDtypeStruct((bsz * n_tok, MIX_W), F32),
                   jax.ShapeDtypeStruct((bsz, 2, 1, MIX_W), F32)],
        compiler_params=pltpu.CompilerParams(
            dimension_semantics=("parallel", "arbitrary"), vmem_limit_bytes=VMEM_LIMIT),
        name="rglru",
    )(*args)


def _unit_lower_inverses(mats):
    wide = PACK * CHUNK
    i = lax.broadcasted_iota(jnp.int32, (CHUNK, wide), 0)
    j = lax.broadcasted_iota(jnp.int32, (CHUNK, wide), 1)
    eye = jnp.where(i == j % CHUNK, 1.0, 0.0)
    col_blk = j // CHUNK

    def block_diag(x):
        return jnp.concatenate([jnp.where(col_blk == b, x, jnp.zeros_like(x)) for b in range(PACK)], axis=0)

    steps = CHUNK.bit_length() - 2
    ps = [-jnp.concatenate(mats[g:g + PACK], axis=1) for g in range(0, len(mats), PACK)]
    ts = [eye + p for p in ps]
    halves = [_split2(p) for p in ps]
    diags = [(block_diag(hi), block_diag(lo)) for hi, lo in halves]
    for step in range(steps):
        if step < NEUMANN_SPLIT_STEPS:
            ps = [_dot_split(hp, dg) for hp, dg in zip(halves, diags)]
            halves = [_split2(p) for p in ps]
            diags = [(block_diag(hi), block_diag(lo)) for hi, lo in halves]
            ts = [t + _dot_split(_split2(t), dg) for t, dg in zip(ts, diags)]
        else:
            ps = [_dot(hp[0], dg[0]) for hp, dg in zip(halves, diags)]
            halves = [(p.astype(BF16), None) for p in ps]
            diags = [(block_diag(hp[0]), None) for hp in halves]
            ts = [t + _dot(t.astype(BF16), dg[0]) for t, dg in zip(ts, diags)]
    return [t[:, b * CHUNK:(b + 1) * CHUNK] for t in ts for b in range(PACK)]


def _gdn_kernel(gp_ref, q_ref, k_ref, v_ref, ab_ref, wq_ref, wk_ref, wv_ref, g_ref,
                *rest, n_tok, seqs, has_init):
    rest = list(rest)
    s0_ref = rest.pop(0) if has_init else None
    o_ref, sf_ref, qt_ref, kw_ref, bm_ref, egl_ref = rest
    h = pl.program_id(1)
    nc = n_tok // CHUNK
    n_rows = seqs * n_tok
    n_chunks = seqs * nc
    group = min(GROUP, n_chunks)
    lane = lax.broadcasted_iota(jnp.int32, (CHUNK, LANES), 1)
    masks = {d: (_order_mask(CHUNK, d == 1, strict=False), _order_mask(CHUNK, d == 1, strict=True))
             for d in range(2)}

    def front(i, carry):
        chunks = [i * group + j for j in range(group)]
        rows = [_rows(c) for c in chunks]
        q, k, v, ab = [], [], [], []
        for c, r in zip(chunks, rows):
            qc = _silu(_conv_chunk(q_ref, c, nc, n_rows, wq_ref[...]))
            kc = _silu(_conv_chunk(k_ref, c, nc, n_rows, wk_ref[...]))
            q.append(qc * (lax.rsqrt(jnp.sum(qc * qc, axis=-1, keepdims=True) + EPS) * D_HEAD ** -0.5))
            k.append(kc * lax.rsqrt(jnp.sum(kc * kc, axis=-1, keepdims=True) + EPS))
            v.append(_silu(_conv_chunk(v_ref, c, nc, n_rows, wv_ref[...])))
            x_ab = ab_ref[r, :]
            ab.append((gp_ref[0:1, :] * jax.nn.softplus(x_ab + gp_ref[1:2, :]), jax.nn.sigmoid(x_ab)))
        ctx = [(j, d) for j in range(group) for d in range(2)]
        beta, both = [], []
        for j, d in ctx:
            incl, strict = masks[d]
            g = jnp.sum(jnp.where(lane == d * N_HEAD + h, ab[j][0], 0.0), axis=1, keepdims=True)
            beta.append(jnp.sum(jnp.where(lane == (2 + d) * N_HEAD + h, ab[j][1], 0.0), axis=1, keepdims=True))
            rhs = jnp.concatenate([jnp.broadcast_to(g, (CHUNK, CHUNK)), g * strict.astype(F32)], axis=1)
            both.append(_dot_mask(incl.astype(BF16), rhs))
        qk_kk = [_dot_nt(jnp.concatenate([q[j], k[j]], axis=0), k[j]) for j in range(group)]
        gc, gam, a_mat = [], [], []
        for m, (j, d) in enumerate(ctx):
            incl, strict = masks[d]
            gc.append(jnp.broadcast_to(both[m][:, 0:1], (CHUNK, LANES)))
            gam.append(jnp.where(incl, jnp.exp(jnp.where(incl, both[m][:, CHUNK:], 0.0)), 0.0))
            a_mat.append(jnp.where(strict, beta[m] * qk_kk[j][CHUNK:] * gam[m], 0.0))
        t_inv = _unit_lower_inverses(a_mat)
        e_gc = [jnp.exp(x) for x in gc]
        wu = [_dot(t_inv[m], jnp.concatenate([k[j] * (beta[m] * e_gc[m]), v[j] * beta[m]], axis=1))
              for m, (j, d) in enumerate(ctx)]
        a_wu = [_dot(qk_kk[j][:CHUNK] * gam[m], wu[m]) for m, (j, d) in enumerate(ctx)]
        gl = [gc[m][(0 if d == 1 else CHUNK - 1):(1 if d == 1 else CHUNK), :] for m, (j, d) in enumerate(ctx)]
        kd_wu = [_dot_tn(k[j] * jnp.exp(gl[m] - gc[m]), wu[m]) for m, (j, d) in enumerate(ctx)]
        for m, (j, d) in enumerate(ctx):
            qt_ref[d, rows[j], :] = (q[j] * e_gc[m] - a_wu[m][:, :D_HEAD]).astype(BF16)
            kw_ref[d, chunks[j]] = kd_wu[m][:, :D_HEAD].astype(BF16)
            bm_ref[d, chunks[j]] = kd_wu[m][:, D_HEAD:]
            egl_ref[d, chunks[j]] = jnp.broadcast_to(jnp.exp(gl[m]), (SUBLANES, LANES))
        for j in range(group):
            o_ref[rows[j], :] = a_wu[2 * j][:, D_HEAD:] + a_wu[2 * j + 1][:, D_HEAD:]
        return carry

    lax.fori_loop(0, n_chunks // group, front, 0)

    unroll = 2 if seqs == 1 else 1
    chains = [(sq, d) for sq in range(seqs) for d in range(2)]

    def back(i, carry):
        s = list(carry)
        for u in range(unroll):
            step = i * unroll + u
            for n, (sq, d) in enumerate(chains):
                c = sq * nc + ((nc - 1 - step) if d == 1 else step)
                rows = _rows(c)
                s_bf = s[n].astype(BF16)
                o_ref[rows, :] = o_ref[rows, :] + _dot(qt_ref[d, rows, :], s_bf)
                s[n] = s[n] * egl_ref[d, c][0:1, :] - _dot(kw_ref[d, c], s_bf) + bm_ref[d, c]
        return tuple(s)

    zero = jnp.zeros((D_HEAD, D_HEAD), F32)
    s0 = tuple(s0_ref[sq, d] if has_init else zero for sq, d in chains)
    s_fin = lax.fori_loop(0, nc // unroll, back, s0)
    for n, (sq, d) in enumerate(chains):
        sf_ref[sq, d] = s_fin[n]
    _norm_rows(o_ref, n_rows, g_ref)


def _gated_delta(proj, neg_a, dt_bias, conv_w, norm_g, state, layer, bsz, n_tok):
    seqs = max(1, min(bsz, GROUP * CHUNK // n_tok))
    rows = seqs * n_tok

    def col_spec(cb):
        return pl.BlockSpec((rows, LANES), lambda b, h: (b, cb + h))

    in_specs = [pl.BlockSpec((2, LANES), lambda b, h: (0, 0)),
                col_spec(CB_CQ), col_spec(CB_CK), col_spec(CB_CV),
                pl.BlockSpec((rows, LANES), lambda b, h: (b, CB_CAB)),
                pl.BlockSpec((CONV_K, LANES), lambda b, h: (0, h)),
                pl.BlockSpec((CONV_K, LANES), lambda b, h: (0, N_HEAD + h)),
                pl.BlockSpec((CONV_K, LANES), lambda b, h: (0, 2 * N_HEAD + h)),
                pl.BlockSpec((1, D_HEAD), lambda b, h: (0, 0))]
    gate_par = jnp.zeros((2, LANES), F32).at[0, :2 * N_HEAD].set(neg_a.reshape(-1)).at[1, :2 * N_HEAD].set(
        dt_bias.reshape(-1))
    args = [gate_par, proj, proj, proj, proj, conv_w, conv_w, conv_w, norm_g]
    if state is not None:
        in_specs.append(pl.BlockSpec((seqs, None, 2, None, D_HEAD, D_HEAD),
                                     lambda b, h: (b, layer, 0, h, 0, 0)))
        args.append(state)
    kernel = functools.partial(_gdn_kernel, n_tok=n_tok, seqs=seqs, has_init=state is not None)
    n_chunks = rows // CHUNK
    scratch = [pltpu.VMEM((2, rows, D_HEAD), BF16),
               pltpu.VMEM((2, n_chunks, D_HEAD, D_HEAD), BF16),
               pltpu.VMEM((2, n_chunks, D_HEAD, D_HEAD), F32),
               pltpu.VMEM((2, n_chunks, SUBLANES, LANES), F32)]
    return _head_mixer_call(kernel, "gated_delta", bsz, n_tok, in_specs, args, scratch, seqs=seqs)


def _block_rows(x, row):
    nb = CHUNK // SUB
    x3 = x.reshape(nb, SUB, LANES)[:, row:row + 1, :]
    return jnp.broadcast_to(x3, (nb, SUB, LANES)).reshape(CHUNK, LANES)


def _hgrn_kernel(q_ref, f0_ref, f1_ref, v_ref, lb_ref, g_ref, *rest, n_tok, has_init):
    rest = list(rest)
    s0_ref = rest.pop(0) if has_init else None
    o_ref, sf_ref = rest
    nc = n_tok // CHUNK
    nb = CHUNK // SUB
    blk_c = lax.broadcasted_iota(jnp.int32, (CHUNK, 1), 0) // SUB
    blk_i = lax.broadcasted_iota(jnp.int32, (CHUNK, CHUNK), 0) // SUB
    blk_j = lax.broadcasted_iota(jnp.int32, (CHUNK, CHUNK), 1) // SUB
    _zero_rows(o_ref, n_tok)

    def body(i, carry):
        ctx = [(d, (nc - 1 - (i * GROUP_D + j)) if d == 1 else i * GROUP_D + j)
               for j in range(GROUP_D) for d in range(2)]
        n = len(ctx)
        incl = {d: _order_mask(CHUNK, d == 1, strict=False) for d in range(2)}
        rows = [_rows(cc) for _, cc in ctx]
        q = [_silu(q_ref[r, :]) * D_HEAD ** -0.5 for r in rows]
        v = [v_ref[r, :] for r in rows]
        k, log_f, bc = [], [], []
        for (d, _), r in zip(ctx, rows):
            zf = (f1_ref if d == 1 else f0_ref)[r, :]
            lb = lb_ref[d:d + 1, :]
            k.append((1.0 - lb) * jax.nn.sigmoid(-zf))
            f_gate = lb + (1.0 - lb) * jax.nn.sigmoid(zf)
            log_f.append(jnp.log(jnp.maximum(f_gate, 1e-30)))
            bc.append(_dot_mask(incl[d].astype(BF16), log_f[-1]))
        ex = [b - lf for b, lf in zip(bc, log_f)]
        a_diag = []
        for m, (d, _) in enumerate(ctx):
            m_blk = _block_rows(bc[m], SUB // 2 if d == 1 else SUB // 2 - 1)
            a_diag.append(_dot_nt(q[m] * jnp.exp(bc[m] - m_blk), k[m] * jnp.exp(m_blk - bc[m])))
        a_mat = []
        for m, (d, _) in enumerate(ctx):
            first = SUB - 1 if d == 1 else 0
            q_c = q[m] * jnp.exp(bc[m] - _block_rows(ex[m], first))
            cross = []
            for blk in range(nb):
                if blk == (nb - 1 if d == 1 else 0):
                    cross.append(jnp.zeros((SUB, CHUNK), F32))
                    continue
                earlier = (blk_c > blk) if d == 1 else (blk_c < blk)
                r_i = ex[m][blk * SUB + first:blk * SUB + first + 1, :]
                k_t = jnp.where(earlier, k[m] * jnp.exp(jnp.where(earlier, r_i - bc[m], 0.0)), 0.0)
                cross.append(_dot_nt(q_c[blk * SUB:(blk + 1) * SUB, :], k_t))
            diag = incl[d] & (blk_i == blk_j)
            a_mat.append(jnp.where(diag, a_diag[m], 0.0) + jnp.concatenate(cross, axis=0))
        b_last = [bc[m][(0 if d == 1 else CHUNK - 1):(1 if d == 1 else CHUNK), :]
                  for m, (d, _) in enumerate(ctx)]
        kv = [_dot_tn(v[m], k[m] * jnp.exp(b_last[m] - bc[m])) for m in range(n)]
        intra = [_dot(a_mat[m], v[m]) for m in range(n)]
        st = list(carry)
        for m, (d, _) in enumerate(ctx):
            o = intra[m] + _dot_nt(q[m] * jnp.exp(bc[m]), st[d])
            o_ref[rows[m], :] = o_ref[rows[m], :] + o
            st[d] = st[d] * jnp.exp(b_last[m]) + kv[m]
        return tuple(st)

    zero = jnp.zeros((D_HEAD, D_HEAD), F32)
    st0 = (s0_ref[0].T, s0_ref[1].T) if has_init else (zero, zero)
    st_f, st_b = lax.fori_loop(0, nc // GROUP_D, body, st0)
    sf_ref[0] = st_f.T
    sf_ref[1] = st_b.T
    _norm_rows(o_ref, n_tok, g_ref)


def _hgrn2(proj, lower_bound, norm_g, state, layer, bsz, n_tok):
    in_specs = [_col_spec(n_tok, CB_DQ), _col_spec(n_tok, CB_DF), _col_spec(n_tok, CB_DF + N_HEAD),
                _col_spec(n_tok, CB_DI),
                pl.BlockSpec((2, D_HEAD), lambda b, h: (0, h)),
                pl.BlockSpec((1, D_HEAD), lambda b, h: (0, 0))]
    args = [proj, proj, proj, proj, lower_bound, norm_g]
    if state is not None:
        in_specs.append(_state_specs(layer)[0])
        args.append(state)
    kernel = functools.partial(_hgrn_kernel, n_tok=n_tok, has_init=state is not None)
    return _head_mixer_call(kernel, "hgrn2", bsz, n_tok, in_specs, args)


def _rope_tables(n_tok):
    n_freq = D_HEAD // 4
    inv = ROPE_BASE ** (-jnp.arange(n_freq, dtype=F32) / n_freq)
    rows = n_tok // GRID_W
    r = jnp.repeat(jnp.arange(rows, dtype=F32), GRID_W)
    col = jnp.tile(jnp.arange(GRID_W, dtype=F32), rows)
    ang = jnp.concatenate([r[:, None] * inv, col[:, None] * inv], axis=-1)
    cos, sin = jnp.cos(ang), jnp.sin(ang)
    return jnp.concatenate([cos, cos], axis=-1), jnp.concatenate([-sin, sin], axis=-1)


def _split_w_in(w):
    m = MIX_W
    a = w[:, 0:4 * m]
    b = w[:, 4 * m:6 * m]
    c = w[:, 6 * m:10 * m + 16]
    dd = w[:, 10 * m + 16:15 * m + 16]
    mg = w[:, 15 * m + 16:]
    cab = jnp.pad(c[:, 4 * m:], ((0, 0), (0, LANES - 16)))
    w1 = jnp.concatenate([a[:, :3 * m], b[:, :m], c[:, :3 * m], dd[:, :4 * m], cab], axis=1)
    wz = jnp.stack([a[:, 3 * m:], b[:, m:], c[:, 3 * m:4 * m], dd[:, 4 * m:]], axis=0)
    wg = jnp.moveaxis(mg.reshape(D_MODEL, N_BRANCH, D_MODEL), 1, 0)
    return w1.astype(BF16), wz.astype(BF16), wg.astype(BF16)


def _pair_block_diag(gw):
    z = jnp.zeros_like(gw[:, :, 0::2])
    top = jnp.concatenate([gw[:, :, 0::2], z], axis=-1)
    bot = jnp.concatenate([z, gw[:, :, 1::2]], axis=-1)
    return jnp.concatenate([top, bot], axis=-2)


def kernel(x_prompt, x_sample, state_ret, state_lru, state_gdn, state_hgrn, c, c_ctx, norm_g,
           w_mod, b_mod, w_in, ret_decay, lru_conv_w, lru_conv_b, lru_gate_w, lru_gate_b,
           lru_lambda, gdn_conv_w, gdn_a_log, gdn_dt_bias, gdn_norm_g, hgrn_lb, hgrn_norm_g,
           w_branch, w_out, final_norm_g):
    bp, n_ctx, _ = x_prompt.shape
    bs, n_lat, _ = x_sample.shape

    lb_sm = jax.nn.softmax(hgrn_lb.astype(F32), axis=0)
    lower_bounds = jnp.cumsum(lb_sm, axis=0) - lb_sm[0]
    log_gamma = jnp.log1p(-jnp.exp(ret_decay.astype(F32)))
    gdn_neg_a = -jnp.exp(gdn_a_log.astype(F32))
    rope = _rope_tables(n_lat)
    state_lru5 = state_lru.reshape(bs, DEPTH, 2, 1, MIX_W)
    layer_w = [(*_split_w_in(w_in[l]), w_branch[l].astype(BF16), w_out[l].astype(BF16),
                _pair_block_diag(lru_gate_w[l])) for l in range(DEPTH)]

    mvecs = jnp.concatenate([c_ctx[None, :], c, jnp.zeros((SUBLANES - 1 - bs, D_MODEL), F32)], axis=0)
    mods = _mod_vectors(mvecs, w_mod, b_mod).reshape(DEPTH, SUBLANES, 3, D_MODEL)

    tm, tm_in = 512, 1024
    groups = [
        dict(x=x_prompt.reshape(bp * n_ctx, D_MODEL), bsz=bp, n_tok=n_ctx, rope=None, cached=False,
             mod_row=lambda i, t: 0),
        dict(x=x_sample.reshape(bs * n_lat, D_MODEL), bsz=bs, n_tok=n_lat, rope=rope, cached=True,
             mod_row=lambda i, t: 1 + (i * t) // n_lat),
    ]
    new_states = [[], [], [], []]
    finals = []
    for grp in groups:
        x, bsz, n_tok = grp["x"], grp["bsz"], grp["n_tok"]
        cached = grp["cached"]
        for l in range(DEPTH):
            w1, wz, wg, wb, wo, gate_w = layer_w[l]
            g = norm_g[l][None, :]
            proj = _in_projection(x, mods[l], functools.partial(grp["mod_row"], t=tm_in), g, w1, tm_in)
            o_a, s_a = _retention(proj, log_gamma[l], grp["rope"], state_ret if cached else None,
                                  l, bsz, n_tok)
            gate_b = lru_gate_b[l].reshape(4, 1, MIX_W)
            o_b, s_b = _rglru(proj, lru_conv_w[l], lru_conv_b[l][None, :], gate_w, gate_b,
                              lru_lambda[l][:, None, :], state_lru5 if cached else None, l, bsz, n_tok)
            o_c, s_c = _gated_delta(proj, gdn_neg_a[l], gdn_dt_bias[l].astype(F32), gdn_conv_w[l],
                                    gdn_norm_g[l][None, :], state_gdn if cached else None, l, bsz, n_tok)
            o_d, s_d = _hgrn2(proj, lower_bounds[l], hgrn_norm_g[l][None, :],
                              state_hgrn if cached else None, l, bsz, n_tok)
            final_g = final_norm_g[None, :] if l == DEPTH - 1 else None
            x, y = _merge(x, mods[l], functools.partial(grp["mod_row"], t=tm), g, (o_a, o_b, o_c, o_d),
                          wz, wg, wb, wo, final_g, tm)
            if not cached:
                for acc, s in zip(new_states, (s_a, s_b, s_c, s_d)):
                    acc.append(s)
        finals.append(y)

    y_prompt = finals[0].reshape(bp, n_ctx, D_MODEL)
    y_sample = finals[1].reshape(bs, n_lat, D_MODEL)
    new_ret = jnp.stack(new_states[0], axis=1)
    new_lru = jnp.stack(new_states[1], axis=1).reshape(bp, DEPTH, 2, MIX_W)
    new_gdn = jnp.stack(new_states[2], axis=1)
    new_hgrn = jnp.stack(new_states[3], axis=1)
    return (y_prompt, y_sample, new_ret, new_lru, new_gdn, new_hgrn)
```

```python
import functools

import jax
import jax.numpy as jnp
from jax import lax
from jax.experimental import pallas as pl
from jax.experimental.pallas import tpu as pltpu

F32 = jnp.float32
BF16 = jnp.bfloat16

D_MODEL = 1024
DEPTH = 4
MIX_W = 512
N_BRANCH = 4
N_HEAD = 4
D_HEAD = 128
GRID_W = 64
LRU_C = 8.0
CONV_K = 4
CONV_LEFT = 2
CHUNK = 64
RET_CHUNK = 256
SUB = 16
GROUP = 16
GROUP_D = 8
LRU_UNROLL = 4
PACK = 4
NEUMANN_SPLIT_STEPS = 3
ROPE_BASE = 10000.0
EPS = 1e-6
LANES = 128
SUBLANES = 8
VMEM_LIMIT = 56 * 1024 * 1024

CB_AQ, CB_AK, CB_AV, CB_BX, CB_CQ, CB_CK, CB_CV, CB_DQ, CB_DF, CB_DI, CB_CAB = (
    0, 4, 8, 12, 16, 20, 24, 28, 32, 40, 44)
MIX_COLS = 45 * LANES
TN_IN = 1920


def _dot(a, b):
    return jnp.dot(a, b, preferred_element_type=F32)


def _dot_nt(a, b):
    return lax.dot_general(a, b, (((1,), (1,)), ((), ())), preferred_element_type=F32)


def _dot_tn(a, b):
    return lax.dot_general(a, b, (((0,), (0,)), ((), ())), preferred_element_type=F32)


def _split2(x):
    hi = x.astype(BF16)
    lo = (x - hi.astype(F32)).astype(BF16)
    return hi, lo


def _dot_split(a, b):
    return _dot(a[0], b[0]) + (_dot(a[0], b[1]) + _dot(a[1], b[0]))


def _dot_mask(mask_bf16, x):
    hi = x.astype(BF16)
    r1 = x - hi.astype(F32)
    mid = r1.astype(BF16)
    lo = (r1 - mid.astype(F32)).astype(BF16)
    return _dot(mask_bf16, hi) + (_dot(mask_bf16, mid) + _dot(mask_bf16, lo))


def _order_mask(n, reverse, strict):
    i = lax.broadcasted_iota(jnp.int32, (n, n), 0)
    j = lax.broadcasted_iota(jnp.int32, (n, n), 1)
    if reverse:
        return (j > i) if strict else (j >= i)
    return (j < i) if strict else (j <= i)


def _rows(cc, size=CHUNK):
    return pl.ds(pl.multiple_of(cc * size, size), size)


def _conv_chunk(x_ref, cc, nc, n_rows, w):
    r0 = pl.multiple_of(cc * CHUNK, CHUNK)
    cur = x_ref[pl.ds(r0, CHUNK), :]
    p0 = pl.multiple_of(jnp.maximum(r0 - SUBLANES, 0), SUBLANES)
    n0 = pl.multiple_of(jnp.minimum(r0 + CHUNK, n_rows - SUBLANES), SUBLANES)
    in_seq = cc % nc
    prev = x_ref[pl.ds(p0, SUBLANES), :] * jnp.where(in_seq > 0, 1.0, 0.0)
    nxt = x_ref[pl.ds(n0, SUBLANES), :] * jnp.where(in_seq < nc - 1, 1.0, 0.0)
    xw = jnp.concatenate([prev, cur, nxt], axis=0)
    rows = CHUNK + 2 * SUBLANES
    acc = None
    for j in range(CONV_K):
        shift = (CONV_LEFT - j) % rows
        rolled = xw if shift == 0 else pltpu.roll(xw, shift, 0)
        term = rolled[SUBLANES:SUBLANES + CHUNK, :] * w[j:j + 1, :]
        acc = term if acc is None else acc + term
    return acc


def _silu(x):
    return x * jax.nn.sigmoid(x)


def _rms(x):
    return x * lax.rsqrt(jnp.mean(x * x, axis=-1, keepdims=True) + EPS)


def _mod_kernel(m_ref, w_ref, b_ref, o_ref):
    m = _silu(m_ref[...])
    o_ref[...] = _dot(m.astype(BF16), w_ref[...].astype(BF16)) + b_ref[...]


def _mod_vectors(mvecs, w_mod, b_mod):
    tn = D_MODEL
    return pl.pallas_call(
        _mod_kernel,
        grid=(DEPTH, 3 * D_MODEL // tn),
        in_specs=[
            pl.BlockSpec((SUBLANES, D_MODEL), lambda l, j: (0, 0)),
            pl.BlockSpec((None, D_MODEL, tn), lambda l, j: (l, 0, j)),
            pl.BlockSpec((None, 1, tn), lambda l, j: (l, 0, j)),
        ],
        out_specs=pl.BlockSpec((None, SUBLANES, tn), lambda l, j: (l, 0, j)),
        out_shape=jax.ShapeDtypeStruct((DEPTH, SUBLANES, 3 * D_MODEL), F32),
        compiler_params=pltpu.CompilerParams(vmem_limit_bytes=VMEM_LIMIT),
        name="mod_vectors",
    )(mvecs, w_mod, b_mod.reshape(DEPTH, 1, 3 * D_MODEL))


def _modulated_norm(x, g, mod):
    return _rms(x) * g * (1.0 + mod[1:2, :]) + mod[0:1, :]


def _inproj_kernel(x_ref, mod_ref, g_ref, w_ref, o_ref, h_ref):
    @pl.when(pl.program_id(1) == 0)
    def _():
        h_ref[...] = _modulated_norm(x_ref[...], g_ref[...], mod_ref[...]).astype(BF16)

    o_ref[...] = _dot(h_ref[...], w_ref[...])


def _in_projection(x, mods, mod_row, g, w1, tm):
    t = x.shape[0]
    return pl.pallas_call(
        _inproj_kernel,
        grid=(t // tm, MIX_COLS // TN_IN),
        in_specs=[
            pl.BlockSpec((tm, D_MODEL), lambda i, j: (i, 0)),
            pl.BlockSpec((None, 3, D_MODEL), lambda i, j: (mod_row(i), 0, 0)),
            pl.BlockSpec((1, D_MODEL), lambda i, j: (0, 0)),
            pl.BlockSpec((D_MODEL, TN_IN), lambda i, j: (0, j)),
        ],
        out_specs=pl.BlockSpec((tm, TN_IN), lambda i, j: (i, j)),
        out_shape=jax.ShapeDtypeStruct((t, MIX_COLS), F32),
        scratch_shapes=[pltpu.VMEM((tm, D_MODEL), BF16)],
        compiler_params=pltpu.CompilerParams(
            dimension_semantics=("parallel", "arbitrary"), vmem_limit_bytes=VMEM_LIMIT),
        name="in_projection",
    )(x, mods, g, w1)


def _merge_kernel(x_ref, mod_ref, g_ref, oa_ref, ob_ref, oc_ref, od_ref, wz_ref, wg_ref,
                  wb_ref, wo_ref, *rest, final):
    if final:
        fg_ref, o_ref, y_ref = rest
    else:
        (o_ref,) = rest
    x = x_ref[...]
    mod = mod_ref[...]
    h = _modulated_norm(x, g_ref[...], mod).astype(BF16)
    acc = None
    for k, b_ref in enumerate((oa_ref, ob_ref, oc_ref, od_ref)):
        z = _dot(h, wz_ref[k])
        br = (b_ref[...] * _silu(z)).astype(BF16)
        yb = _dot(br, wb_ref[k])
        term = jax.nn.sigmoid(_dot(h, wg_ref[k])) * yb
        acc = term if acc is None else acc + term
    y = _dot(acc.astype(BF16), wo_ref[...])
    out = x + mod[2:3, :] * y
    o_ref[...] = out
    if final:
        y_ref[...] = _rms(out) * fg_ref[...]


def _merge(x, mods, mod_row, g, outs, wz, wg, wb, wo, final_g, tm):
    t = x.shape[0]
    final = final_g is not None
    const = dict(pipeline_mode=pl.Buffered(1))
    in_specs = [
        pl.BlockSpec((tm, D_MODEL), lambda i: (i, 0)),
        pl.BlockSpec((None, 3, D_MODEL), lambda i: (mod_row(i), 0, 0)),
        pl.BlockSpec((1, D_MODEL), lambda i: (0, 0)),
    ] + [pl.BlockSpec((tm, MIX_W), lambda i: (i, 0)) for _ in range(N_BRANCH)] + [
        pl.BlockSpec((N_BRANCH, D_MODEL, MIX_W), lambda i: (0, 0, 0), **const),
        pl.BlockSpec((N_BRANCH, D_MODEL, D_MODEL), lambda i: (0, 0, 0), **const),
        pl.BlockSpec((N_BRANCH, MIX_W, D_MODEL), lambda i: (0, 0, 0), **const),
        pl.BlockSpec((D_MODEL, D_MODEL), lambda i: (0, 0), **const),
    ]
    args = [x, mods, g, *outs, wz, wg, wb, wo]
    out_specs = [pl.BlockSpec((tm, D_MODEL), lambda i: (i, 0))]
    out_shape = [jax.ShapeDtypeStruct((t, D_MODEL), F32)]
    if final:
        in_specs.append(pl.BlockSpec((1, D_MODEL), lambda i: (0, 0)))
        args.append(final_g)
        out_specs.append(pl.BlockSpec((tm, D_MODEL), lambda i: (i, 0)))
        out_shape.append(jax.ShapeDtypeStruct((t, D_MODEL), F32))
    res = pl.pallas_call(
        functools.partial(_merge_kernel, final=final),
        grid=(t // tm,),
        in_specs=in_specs,
        out_specs=out_specs,
        out_shape=out_shape,
        compiler_params=pltpu.CompilerParams(
            dimension_semantics=("parallel",), vmem_limit_bytes=VMEM_LIMIT),
        name="merge",
    )(*args)
    return res if final else (res[0], None)


def _state_specs(layer):
    s_in = pl.BlockSpec((None, None, 2, None, D_HEAD, D_HEAD), lambda b, h: (b, layer, 0, h, 0, 0))
    s_out = pl.BlockSpec((None, 2, None, D_HEAD, D_HEAD), lambda b, h: (b, 0, h, 0, 0))
    return s_in, s_out


def _col_spec(n_tok, cb):
    return pl.BlockSpec((n_tok, LANES), lambda b, h: (b, cb + h))


def _head_mixer_call(kernel, name, bsz, n_tok, in_specs, args, scratch=(), seqs=None):
    _, s_out = _state_specs(0)
    if seqs is not None:
        s_out = pl.BlockSpec((seqs, 2, None, D_HEAD, D_HEAD), lambda b, h: (b, 0, h, 0, 0))
    per_step = seqs or 1
    return pl.pallas_call(
        kernel,
        grid=(bsz // per_step, N_HEAD),
        in_specs=in_specs,
        out_specs=[pl.BlockSpec((per_step * n_tok, LANES), lambda b, h: (b, h)), s_out],
        out_shape=[jax.ShapeDtypeStruct((bsz * n_tok, MIX_W), F32),
                   jax.ShapeDtypeStruct((bsz, 2, N_HEAD, D_HEAD, D_HEAD), F32)],
        scratch_shapes=list(scratch),
        compiler_params=pltpu.CompilerParams(
            dimension_semantics=("parallel", "arbitrary"), vmem_limit_bytes=VMEM_LIMIT),
        name=name,
    )(*args)


def _zero_rows(o_ref, n_tok):
    def body(c, carry):
        o_ref[_rows(c), :] = jnp.zeros((CHUNK, LANES), F32)
        return carry

    lax.fori_loop(0, n_tok // CHUNK, body, 0)


def _norm_rows(o_ref, n_tok, g_ref):
    def body(c, carry):
        o_ref[_rows(c), :] = _rms(o_ref[_rows(c), :]) * g_ref[...]
        return carry

    lax.fori_loop(0, n_tok // CHUNK, body, 0)


def _ret_kernel(lg_ref, q_ref, k_ref, v_ref, *rest, n_tok, rope, has_init):
    rest = list(rest)
    cos_ref, sin_ref = (rest.pop(0), rest.pop(0)) if rope else (None, None)
    s0_ref = rest.pop(0) if has_init else None
    o_ref, sf_ref, kvb_ref = rest
    h = pl.program_id(1)
    size = RET_CHUNK
    nc = n_tok // size
    lg_f, lg_b = lg_ref[0, h], lg_ref[1, h]
    pos_c = lax.broadcasted_iota(jnp.int32, (size, 1), 0).astype(F32)
    pos_r = lax.broadcasted_iota(jnp.int32, (1, size), 1).astype(F32)
    rel = pos_c - pos_r
    lower, upper = rel >= 0, rel <= 0
    dsum = (jnp.where(lower, jnp.exp(jnp.where(lower, rel, 0.0) * lg_f), 0.0)
            + jnp.where(upper, jnp.exp(jnp.where(upper, -rel, 0.0) * lg_b), 0.0))
    q_dec_f = jnp.exp((pos_c + 1.0) * lg_f)
    k_dec_f = jnp.exp((size - 1.0 - pos_c) * lg_f)
    q_dec_b = jnp.exp((size - pos_c) * lg_b)
    k_dec_b = jnp.exp(pos_c * lg_b)
    c_dec_f = jnp.exp(jnp.full((1, D_HEAD), size * lg_f, F32))
    c_dec_b = jnp.exp(jnp.full((1, D_HEAD), size * lg_b, F32))

    def rotate(x, rows):
        if not rope:
            return x
        return x * cos_ref[rows, :] + pltpu.roll(x, D_HEAD // 2, 1) * sin_ref[rows, :]

    unroll = 2 if nc % 2 == 0 else 1

    def forward(i, s):
        cs = [i * unroll + j for j in range(unroll)]
        rows = [_rows(c, size) for c in cs]
        q = [rotate(q_ref[r, :], r) for r in rows]
        k = [rotate(k_ref[r, :], r) * D_HEAD ** -0.5 for r in rows]
        v = [v_ref[r, :] for r in rows]
        scores = [_dot_nt(qj, kj) for qj, kj in zip(q, k)]
        kv_b = [_dot_tn(kj * k_dec_b, vj) for kj, vj in zip(k, v)]
        kv_f = [_dot_tn(kj * k_dec_f, vj) for kj, vj in zip(k, v)]
        inter = []
        for j in range(unroll):
            inter.append(_dot(q[j] * q_dec_f, s))
            s = s * c_dec_f + kv_f[j]
        for j in range(unroll):
            o_ref[rows[j], :] = _dot(scores[j] * dsum, v[j]) + inter[j]
            kvb_ref[cs[j]] = kv_b[j]
        return s

    s0 = s0_ref[0] if has_init else jnp.zeros((D_HEAD, D_HEAD), F32)
    sf_ref[0] = lax.fori_loop(0, nc // unroll, forward, s0)

    def backward(i, s):
        for j in range(unroll):
            c = nc - 1 - (i * unroll + j)
            rows = _rows(c, size)
            q = rotate(q_ref[rows, :], rows)
            o_ref[rows, :] = _rms(o_ref[rows, :] + _dot(q * q_dec_b, s))
            s = s * c_dec_b + kvb_ref[c]
        return s

    s0 = s0_ref[1] if has_init else jnp.zeros((D_HEAD, D_HEAD), F32)
    sf_ref[1] = lax.fori_loop(0, nc // unroll, backward, s0)


def _retention(proj, log_gamma, rope, state, layer, bsz, n_tok):
    in_specs = [pl.BlockSpec(memory_space=pltpu.SMEM),
                _col_spec(n_tok, CB_AQ), _col_spec(n_tok, CB_AK), _col_spec(n_tok, CB_AV)]
    args = [log_gamma, proj, proj, proj]
    if rope is not None:
        in_specs += [pl.BlockSpec((n_tok, LANES), lambda b, h: (0, 0))] * 2
        args += list(rope)
    if state is not None:
        in_specs.append(_state_specs(layer)[0])
        args.append(state)
    kernel = functools.partial(_ret_kernel, n_tok=n_tok, rope=rope is not None,
                               has_init=state is not None)
    scratch = [pltpu.VMEM((n_tok // RET_CHUNK, D_HEAD, D_HEAD), F32)]
    return _head_mixer_call(kernel, "retention", bsz, n_tok, in_specs, args, scratch)


def _tile_scan(a, u, reverse):
    row = lax.broadcasted_iota(jnp.int32, a.shape, 0)
    for s in (1, 2, 4):
        if reverse:
            valid = row < SUBLANES - s
            shift = SUBLANES - s
        else:
            valid = row >= s
            shift = s
        a_sh = jnp.where(valid, pltpu.roll(a, shift, 0), 1.0)
        u_sh = jnp.where(valid, pltpu.roll(u, shift, 0), 0.0)
        u = a * u_sh + u
        a = a * a_sh
    return a, u


def _lru_kernel(x_ref, cw_ref, cb_ref, gw_ref, gb_ref, lam_ref, *rest, n_tok, has_init):
    rest = list(rest)
    h0_ref = rest.pop(0) if has_init else None
    o_ref, hl_ref = rest
    nc = n_tok // CHUNK
    n_tile = CHUNK // SUBLANES
    cw = cw_ref[...]
    cb = cb_ref[...]
    _zero_rows(o_ref, n_tok)

    unroll = min(LRU_UNROLL, nc)
    coef = [-LRU_C * jax.nn.softplus(-lam_ref[d]) for d in range(2)]

    def body(i, carry):
        ctx = [(d, (nc - 1 - (i * unroll + j)) if d == 1 else i * unroll + j)
               for j in range(unroll) for d in range(2)]
        xb = [_conv_chunk(x_ref, cc, nc, n_tok, cw) + cb for _, cc in ctx]
        pre = [(_dot(x, gw_ref[d, 0]), _dot(x, gw_ref[d, 1])) for x, (d, _) in zip(xb, ctx)]
        carry = list(carry)
        for m, (d, cc) in enumerate(ctx):
            r = jax.nn.sigmoid(pre[m][0] + gb_ref[2 * d])
            gate = jax.nn.sigmoid(pre[m][1] + gb_ref[2 * d + 1])
            log_a = coef[d] * r
            a = jnp.exp(log_a)
            one_minus_a2 = -jnp.tanh(log_a) * (a * a + 1.0)
            u = jnp.sqrt(jnp.maximum(one_minus_a2, 1e-12)) * (gate * xb[m])
            scans = [_tile_scan(a[t * SUBLANES:(t + 1) * SUBLANES, :], u[t * SUBLANES:(t + 1) * SUBLANES, :],
                                d == 1) for t in range(n_tile)]
            tiles = [None] * n_tile
            last = 0 if d == 1 else SUBLANES - 1
            for t in (range(n_tile - 1, -1, -1) if d == 1 else range(n_tile)):
                a_t, u_t = scans[t]
                tiles[t] = a_t * carry[d] + u_t
                carry[d] = a_t[last:last + 1, :] * carry[d] + u_t[last:last + 1, :]
            rows = _rows(cc)
            o_ref[rows, :] = o_ref[rows, :] + jnp.concatenate(tiles, axis=0)
        return tuple(carry)

    zero = jnp.zeros((1, LANES), F32)
    h0 = (h0_ref[0], h0_ref[1]) if has_init else (zero, zero)
    h_f, h_b = lax.fori_loop(0, nc // unroll, body, h0)
    hl_ref[0] = h_f
    hl_ref[1] = h_b


def _rglru(proj, conv_w, conv_b, gate_w, gate_b, lam, state, layer, bsz, n_tok):
    n_pair = MIX_W // LANES
    in_specs = [
        pl.BlockSpec((n_tok, LANES), lambda b, p: (b, CB_BX + p)),
        pl.BlockSpec((CONV_K, LANES), lambda b, p: (0, p)),
        pl.BlockSpec((1, LANES), lambda b, p: (0, p)),
        pl.BlockSpec((2, 2, None, LANES, LANES), lambda b, p: (0, 0, p, 0, 0)),
        pl.BlockSpec((4, 1, LANES), lambda b, p: (0, 0, p)),
        pl.BlockSpec((2, 1, LANES), lambda b, p: (0, 0, p)),
    ]
    args = [proj, conv_w, conv_b, gate_w, gate_b, lam]
    if state is not None:
        in_specs.append(pl.BlockSpec((None, None, 2, 1, LANES), lambda b, p: (b, layer, 0, 0, p)))
        args.append(state)
    return pl.pallas_call(
        functools.partial(_lru_kernel, n_tok=n_tok, has_init=state is not None),
        grid=(bsz, n_pair),
        in_specs=in_specs,
        out_specs=[pl.BlockSpec((n_tok, LANES), lambda b, p: (b, p)),
                   pl.BlockSpec((None, 2, 1, LANES), lambda b, p: (b, 0, 0, p))],
        out_shape=[jax.ShapeDtypeStruct((bsz * n_tok, MIX_W), F32),
                   jax.ShapeDtypeStruct((bsz, 2, 1, MIX_W), F32)],
        compiler_params=pltpu.CompilerParams(
            dimension_semantics=("parallel", "arbitrary"), vmem_limit_bytes=VMEM_LIMIT),
        name="rglru",
    )(*args)


def _unit_lower_inverses(mats):
    wide = PACK * CHUNK
    i = lax.broadcasted_iota(jnp.int32, (CHUNK, wide), 0)
    j = lax.broadcasted_iota(jnp.int32, (CHUNK, wide), 1)
    eye = jnp.where(i == j % CHUNK, 1.0, 0.0)
    col_blk = j // CHUNK

    def block_diag(x):
        return jnp.concatenate([jnp.where(col_blk == b, x, jnp.zeros_like(x)) for b in range(PACK)], axis=0)

    steps = CHUNK.bit_length() - 2
    ps = [-jnp.concatenate(mats[g:g + PACK], axis=1) for g in range(0, len(mats), PACK)]
    ts = [eye + p for p in ps]
    halves = [_split2(p) for p in ps]
    diags = [(block_diag(hi), block_diag(lo)) for hi, lo in halves]
    for step in range(steps):
        if step < NEUMANN_SPLIT_STEPS:
            ps = [_dot_split(hp, dg) for hp, dg in zip(halves, diags)]
            halves = [_split2(p) for p in ps]
            diags = [(block_diag(hi), block_diag(lo)) for hi, lo in halves]
            ts = [t + _dot_split(_split2(t), dg) for t, dg in zip(ts, diags)]
        else:
            ps = [_dot(hp[0], dg[0]) for hp, dg in zip(halves, diags)]
            halves = [(p.astype(BF16), None) for p in ps]
            diags = [(block_diag(hp[0]), None) for hp in halves]
            ts = [t + _dot(t.astype(BF16), dg[0]) for t, dg in zip(ts, diags)]
    return [t[:, b * CHUNK:(b + 1) * CHUNK] for t in ts for b in range(PACK)]


def _gdn_kernel(gp_ref, q_ref, k_ref, v_ref, ab_ref, wq_ref, wk_ref, wv_ref, g_ref,
                *rest, n_tok, seqs, has_init):
    rest = list(rest)
    s0_ref = rest.pop(0) if has_init else None
    o_ref, sf_ref, qt_ref, kw_ref, bm_ref, egl_ref = rest
    h = pl.program_id(1)
    nc = n_tok // CHUNK
    n_rows = seqs * n_tok
    n_chunks = seqs * nc
    group = min(GROUP, n_chunks)
    lane = lax.broadcasted_iota(jnp.int32, (CHUNK, LANES), 1)
    masks = {d: (_order_mask(CHUNK, d == 1, strict=False), _order_mask(CHUNK, d == 1, strict=True))
             for d in range(2)}

    def front(i, carry):
        chunks = [i * group + j for j in range(group)]
        rows = [_rows(c) for c in chunks]
        q, k, v, ab = [], [], [], []
        for c, r in zip(chunks, rows):
            qc = _silu(_conv_chunk(q_ref, c, nc, n_rows, wq_ref[...]))
            kc = _silu(_conv_chunk(k_ref, c, nc, n_rows, wk_ref[...]))
            q.append(qc * (lax.rsqrt(jnp.sum(qc * qc, axis=-1, keepdims=True) + EPS) * D_HEAD ** -0.5))
            k.append(kc * lax.rsqrt(jnp.sum(kc * kc, axis=-1, keepdims=True) + EPS))
            v.append(_silu(_conv_chunk(v_ref, c, nc, n_rows, wv_ref[...])))
            x_ab = ab_ref[r, :]
            ab.append((gp_ref[0:1, :] * jax.nn.softplus(x_ab + gp_ref[1:2, :]), jax.nn.sigmoid(x_ab)))
        ctx = [(j, d) for j in range(group) for d in range(2)]
        beta, both = [], []
        for j, d in ctx:
            incl, strict = masks[d]
            g = jnp.sum(jnp.where(lane == d * N_HEAD + h, ab[j][0], 0.0), axis=1, keepdims=True)
            beta.append(jnp.sum(jnp.where(lane == (2 + d) * N_HEAD + h, ab[j][1], 0.0), axis=1, keepdims=True))
            rhs = jnp.concatenate([jnp.broadcast_to(g, (CHUNK, CHUNK)), g * strict.astype(F32)], axis=1)
            both.append(_dot_mask(incl.astype(BF16), rhs))
        qk_kk = [_dot_nt(jnp.concatenate([q[j], k[j]], axis=0), k[j]) for j in range(group)]
        gc, gam, a_mat = [], [], []
        for m, (j, d) in enumerate(ctx):
            incl, strict = masks[d]
            gc.append(jnp.broadcast_to(both[m][:, 0:1], (CHUNK, LANES)))
            gam.append(jnp.where(incl, jnp.exp(jnp.where(incl, both[m][:, CHUNK:], 0.0)), 0.0))
            a_mat.append(jnp.where(strict, beta[m] * qk_kk[j][CHUNK:] * gam[m], 0.0))
        t_inv = _unit_lower_inverses(a_mat)
        e_gc = [jnp.exp(x) for x in gc]
        wu = [_dot(t_inv[m], jnp.concatenate([k[j] * (beta[m] * e_gc[m]), v[j] * beta[m]], axis=1))
              for m, (j, d) in enumerate(ctx)]
        a_wu = [_dot(qk_kk[j][:CHUNK] * gam[m], wu[m]) for m, (j, d) in enumerate(ctx)]
        gl = [gc[m][(0 if d == 1 else CHUNK - 1):(1 if d == 1 else CHUNK), :] for m, (j, d) in enumerate(ctx)]
        kd_wu = [_dot_tn(k[j] * jnp.exp(gl[m] - gc[m]), wu[m]) for m, (j, d) in enumerate(ctx)]
        for m, (j, d) in enumerate(ctx):
            qt_ref[d, rows[j], :] = (q[j] * e_gc[m] - a_wu[m][:, :D_HEAD]).astype(BF16)
            kw_ref[d, chunks[j]] = kd_wu[m][:, :D_HEAD].astype(BF16)
            bm_ref[d, chunks[j]] = kd_wu[m][:, D_HEAD:]
            egl_ref[d, chunks[j]] = jnp.broadcast_to(jnp.exp(gl[m]), (SUBLANES, LANES))
        for j in range(group):
            o_ref[rows[j], :] = a_wu[2 * j][:, D_HEAD:] + a_wu[2 * j + 1][:, D_HEAD:]
        return carry

    lax.fori_loop(0, n_chunks // group, front, 0)

    unroll = 2 if seqs == 1 else 1
    chains = [(sq, d) for sq in range(seqs) for d in range(2)]

    def back(i, carry):
        s = list(carry)
        for u in range(unroll):
            step = i * unroll + u
            for n, (sq, d) in enumerate(chains):
                c = sq * nc + ((nc - 1 - step) if d == 1 else step)
                rows = _rows(c)
                s_bf = s[n].astype(BF16)
                o_ref[rows, :] = o_ref[rows, :] + _dot(qt_ref[d, rows, :], s_bf)
                s[n] = s[n] * egl_ref[d, c][0:1, :] - _dot(kw_ref[d, c], s_bf) + bm_ref[d, c]
        return tuple(s)

    zero = jnp.zeros((D_HEAD, D_HEAD), F32)
    s0 = tuple(s0_ref[sq, d] if has_init else zero for sq, d in chains)
    s_fin = lax.fori_loop(0, nc // unroll, back, s0)
    for n, (sq, d) in enumerate(chains):
        sf_ref[sq, d] = s_fin[n]
    _norm_rows(o_ref, n_rows, g_ref)


def _gated_delta(proj, neg_a, dt_bias, conv_w, norm_g, state, layer, bsz, n_tok):
    seqs = max(1, min(bsz, GROUP * CHUNK // n_tok))
    rows = seqs * n_tok

    def col_spec(cb):
        return pl.BlockSpec((rows, LANES), lambda b, h: (b, cb + h))

    in_specs = [pl.BlockSpec((2, LANES), lambda b, h: (0, 0)),
                col_spec(CB_CQ), col_spec(CB_CK), col_spec(CB_CV),
                pl.BlockSpec((rows, LANES), lambda b, h: (b, CB_CAB)),
                pl.BlockSpec((CONV_K, LANES), lambda b, h: (0, h)),
                pl.BlockSpec((CONV_K, LANES), lambda b, h: (0, N_HEAD + h)),
                pl.BlockSpec((CONV_K, LANES), lambda b, h: (0, 2 * N_HEAD + h)),
                pl.BlockSpec((1, D_HEAD), lambda b, h: (0, 0))]
    gate_par = jnp.zeros((2, LANES), F32).at[0, :2 * N_HEAD].set(neg_a.reshape(-1)).at[1, :2 * N_HEAD].set(
        dt_bias.reshape(-1))
    args = [gate_par, proj, proj, proj, proj, conv_w, conv_w, conv_w, norm_g]
    if state is not None:
        in_specs.append(pl.BlockSpec((seqs, None, 2, None, D_HEAD, D_HEAD),
                                     lambda b, h: (b, layer, 0, h, 0, 0)))
        args.append(state)
    kernel = functools.partial(_gdn_kernel, n_tok=n_tok, seqs=seqs, has_init=state is not None)
    n_chunks = rows // CHUNK
    scratch = [pltpu.VMEM((2, rows, D_HEAD), BF16),
               pltpu.VMEM((2, n_chunks, D_HEAD, D_HEAD), BF16),
               pltpu.VMEM((2, n_chunks, D_HEAD, D_HEAD), F32),
               pltpu.VMEM((2, n_chunks, SUBLANES, LANES), F32)]
    return _head_mixer_call(kernel, "gated_delta", bsz, n_tok, in_specs, args, scratch, seqs=seqs)


def _block_rows(x, row):
    nb = CHUNK // SUB
    x3 = x.reshape(nb, SUB, LANES)[:, row:row + 1, :]
    return jnp.broadcast_to(x3, (nb, SUB, LANES)).reshape(CHUNK, LANES)


def _hgrn_kernel(q_ref, f0_ref, f1_ref, v_ref, lb_ref, g_ref, *rest, n_tok, seqs, has_init):
    rest = list(rest)
    s0_ref = rest.pop(0) if has_init else None
    o_ref, sf_ref = rest
    nc = n_tok // CHUNK
    nb = CHUNK // SUB
    blk_c = lax.broadcasted_iota(jnp.int32, (CHUNK, 1), 0) // SUB
    blk_i = lax.broadcasted_iota(jnp.int32, (CHUNK, CHUNK), 0) // SUB
    blk_j = lax.broadcasted_iota(jnp.int32, (CHUNK, CHUNK), 1) // SUB
    _zero_rows(o_ref, seqs * n_tok)
    per = max(1, GROUP_D // seqs)

    def body(i, carry):
        ctx = [(2 * sq + d, sq * nc + ((nc - 1 - (i * per + j)) if d == 1 else i * per + j))
               for j in range(per) for sq in range(seqs) for d in range(2)]
        n = len(ctx)
        dirs = [chain % 2 for chain, _ in ctx]
        incl = {d: _order_mask(CHUNK, d == 1, strict=False) for d in range(2)}
        rows = [_rows(cc) for _, cc in ctx]
        q = [_silu(q_ref[r, :]) * D_HEAD ** -0.5 for r in rows]
        v = [v_ref[r, :] for r in rows]
        k, log_f, bc = [], [], []
        for d, r in zip(dirs, rows):
            zf = (f1_ref if d == 1 else f0_ref)[r, :]
            lb = lb_ref[d:d + 1, :]
            k.append((1.0 - lb) * jax.nn.sigmoid(-zf))
            f_gate = lb + (1.0 - lb) * jax.nn.sigmoid(zf)
            log_f.append(jnp.log(jnp.maximum(f_gate, 1e-30)))
            bc.append(_dot_mask(incl[d].astype(BF16), log_f[-1]))
        ex = [b - lf for b, lf in zip(bc, log_f)]
        a_diag = []
        for m, d in enumerate(dirs):
            m_blk = _block_rows(bc[m], SUB // 2 if d == 1 else SUB // 2 - 1)
            a_diag.append(_dot_nt(q[m] * jnp.exp(bc[m] - m_blk), k[m] * jnp.exp(m_blk - bc[m])))
        a_mat = []
        for m, d in enumerate(dirs):
            first = SUB - 1 if d == 1 else 0
            q_c = q[m] * jnp.exp(bc[m] - _block_rows(ex[m], first))
            cross = []
            for blk in range(nb):
                if blk == (nb - 1 if d == 1 else 0):
                    cross.append(jnp.zeros((SUB, CHUNK), F32))
                    continue
                earlier = (blk_c > blk) if d == 1 else (blk_c < blk)
                r_i = ex[m][blk * SUB + first:blk * SUB + first + 1, :]
                k_t = jnp.where(earlier, k[m] * jnp.exp(jnp.where(earlier, r_i - bc[m], 0.0)), 0.0)
                cross.append(_dot_nt(q_c[blk * SUB:(blk + 1) * SUB, :], k_t))
            diag = incl[d] & (blk_i == blk_j)
            a_mat.append(jnp.where(diag, a_diag[m], 0.0) + jnp.concatenate(cross, axis=0))
        b_last = [bc[m][(0 if d == 1 else CHUNK - 1):(1 if d == 1 else CHUNK), :] for m, d in enumerate(dirs)]
        kv = [_dot_tn(v[m], k[m] * jnp.exp(b_last[m] - bc[m])) for m in range(n)]
        intra = [_dot(a_mat[m], v[m]) for m in range(n)]
        st = list(carry)
        for m, (chain, _) in enumerate(ctx):
            o = intra[m] + _dot_nt(q[m] * jnp.exp(bc[m]), st[chain])
            o_ref[rows[m], :] = o_ref[rows[m], :] + o
            st[chain] = st[chain] * jnp.exp(b_last[m]) + kv[m]
        return tuple(st)

    zero = jnp.zeros((D_HEAD, D_HEAD), F32)
    st0 = tuple(s0_ref[sq, d].T if has_init else zero for sq in range(seqs) for d in range(2))
    st_fin = lax.fori_loop(0, nc // per, body, st0)
    for sq in range(seqs):
        for d in range(2):
            sf_ref[sq, d] = st_fin[2 * sq + d].T
    _norm_rows(o_ref, seqs * n_tok, g_ref)


def _hgrn2(proj, lower_bound, norm_g, state, layer, bsz, n_tok):
    seqs = max(1, min(bsz, GROUP_D * CHUNK // n_tok))
    rows = seqs * n_tok
    in_specs = [_col_spec(rows, CB_DQ), _col_spec(rows, CB_DF), _col_spec(rows, CB_DF + N_HEAD),
                _col_spec(rows, CB_DI),
                pl.BlockSpec((2, D_HEAD), lambda b, h: (0, h)),
                pl.BlockSpec((1, D_HEAD), lambda b, h: (0, 0))]
    args = [proj, proj, proj, proj, lower_bound, norm_g]
    if state is not None:
        in_specs.append(pl.BlockSpec((seqs, None, 2, None, D_HEAD, D_HEAD),
                                     lambda b, h: (b, layer, 0, h, 0, 0)))
        args.append(state)
    kernel = functools.partial(_hgrn_kernel, n_tok=n_tok, seqs=seqs, has_init=state is not None)
    return _head_mixer_call(kernel, "hgrn2", bsz, n_tok, in_specs, args, seqs=seqs)


def _rope_tables(n_tok):
    n_freq = D_HEAD // 4
    inv = ROPE_BASE ** (-jnp.arange(n_freq, dtype=F32) / n_freq)
    rows = n_tok // GRID_W
    r = jnp.repeat(jnp.arange(rows, dtype=F32), GRID_W)
    col = jnp.tile(jnp.arange(GRID_W, dtype=F32), rows)
    ang = jnp.concatenate([r[:, None] * inv, col[:, None] * inv], axis=-1)
    cos, sin = jnp.cos(ang), jnp.sin(ang)
    return jnp.concatenate([cos, cos], axis=-1), jnp.concatenate([-sin, sin], axis=-1)


def _split_w_in(w):
    m = MIX_W
    a = w[:, 0:4 * m]
    b = w[:, 4 * m:6 * m]
    c = w[:, 6 * m:10 * m + 16]
    dd = w[:, 10 * m + 16:15 * m + 16]
    mg = w[:, 15 * m + 16:]
    cab = jnp.pad(c[:, 4 * m:], ((0, 0), (0, LANES - 16)))
    w1 = jnp.concatenate([a[:, :3 * m], b[:, :m], c[:, :3 * m], dd[:, :4 * m], cab], axis=1)
    wz = jnp.stack([a[:, 3 * m:], b[:, m:], c[:, 3 * m:4 * m], dd[:, 4 * m:]], axis=0)
    wg = jnp.moveaxis(mg.reshape(D_MODEL, N_BRANCH, D_MODEL), 1, 0)
    return w1.astype(BF16), wz.astype(BF16), wg.astype(BF16)


def _pair_block_diag(gw):
    z = jnp.zeros_like(gw[:, :, 0::2])
    top = jnp.concatenate([gw[:, :, 0::2], z], axis=-1)
    bot = jnp.concatenate([z, gw[:, :, 1::2]], axis=-1)
    return jnp.concatenate([top, bot], axis=-2)


def kernel(x_prompt, x_sample, state_ret, state_lru, state_gdn, state_hgrn, c, c_ctx, norm_g,
           w_mod, b_mod, w_in, ret_decay, lru_conv_w, lru_conv_b, lru_gate_w, lru_gate_b,
           lru_lambda, gdn_conv_w, gdn_a_log, gdn_dt_bias, gdn_norm_g, hgrn_lb, hgrn_norm_g,
           w_branch, w_out, final_norm_g):
    bp, n_ctx, _ = x_prompt.shape
    bs, n_lat, _ = x_sample.shape

    lb_sm = jax.nn.softmax(hgrn_lb.astype(F32), axis=0)
    lower_bounds = jnp.cumsum(lb_sm, axis=0) - lb_sm[0]
    log_gamma = jnp.log1p(-jnp.exp(ret_decay.astype(F32)))
    gdn_neg_a = -jnp.exp(gdn_a_log.astype(F32))
    rope = _rope_tables(n_lat)
    state_lru5 = state_lru.reshape(bs, DEPTH, 2, 1, MIX_W)
    layer_w = [(*_split_w_in(w_in[l]), w_branch[l].astype(BF16), w_out[l].astype(BF16),
                _pair_block_diag(lru_gate_w[l])) for l in range(DEPTH)]

    mvecs = jnp.concatenate([c_ctx[None, :], c, jnp.zeros((SUBLANES - 1 - bs, D_MODEL), F32)], axis=0)
    mods = _mod_vectors(mvecs, w_mod, b_mod).reshape(DEPTH, SUBLANES, 3, D_MODEL)

    tm, tm_in = 512, 1024
    groups = [
        dict(x=x_prompt.reshape(bp * n_ctx, D_MODEL), bsz=bp, n_tok=n_ctx, rope=None, cached=False,
             mod_row=lambda i, t: 0),
        dict(x=x_sample.reshape(bs * n_lat, D_MODEL), bsz=bs, n_tok=n_lat, rope=rope, cached=True,
             mod_row=lambda i, t: 1 + (i * t) // n_lat),
    ]
    new_states = [[], [], [], []]
    finals = []
    for grp in groups:
        x, bsz, n_tok = grp["x"], grp["bsz"], grp["n_tok"]
        cached = grp["cached"]
        for l in range(DEPTH):
            w1, wz, wg, wb, wo, gate_w = layer_w[l]
            g = norm_g[l][None, :]
            proj = _in_projection(x, mods[l], functools.partial(grp["mod_row"], t=tm_in), g, w1, tm_in)
            o_a, s_a = _retention(proj, log_gamma[l], grp["rope"], state_ret if cached else None,
                                  l, bsz, n_tok)
            gate_b = lru_gate_b[l].reshape(4, 1, MIX_W)
            o_b, s_b = _rglru(proj, lru_conv_w[l], lru_conv_b[l][None, :], gate_w, gate_b,
                              lru_lambda[l][:, None, :], state_lru5 if cached else None, l, bsz, n_tok)
            o_c, s_c = _gated_delta(proj, gdn_neg_a[l], gdn_dt_bias[l].astype(F32), gdn_conv_w[l],
                                    gdn_norm_g[l][None, :], state_gdn if cached else None, l, bsz, n_tok)
            o_d, s_d = _hgrn2(proj, lower_bounds[l], hgrn_norm_g[l][None, :],
                              state_hgrn if cached else None, l, bsz, n_tok)
            final_g = final_norm_g[None, :] if l == DEPTH - 1 else None
            x, y = _merge(x, mods[l], functools.partial(grp["mod_row"], t=tm), g, (o_a, o_b, o_c, o_d),
                          wz, wg, wb, wo, final_g, tm)
            if not cached:
                for acc, s in zip(new_states, (s_a, s_b, s_c, s_d)):
                    acc.append(s)
        finals.append(y)

    y_prompt = finals[0].reshape(bp, n_ctx, D_MODEL)
    y_sample = finals[1].reshape(bs, n_lat, D_MODEL)
    new_ret = jnp.stack(new_states[0], axis=1)
    new_lru = jnp.stack(new_states[1], axis=1).reshape(bp, DEPTH, 2, MIX_W)
    new_gdn = jnp.stack(new_states[2], axis=1)
    new_hgrn = jnp.stack(new_states[3], axis=1)
    return (y_prompt, y_sample, new_ret, new_lru, new_gdn, new_hgrn)
```

```python
import functools

import jax
import jax.numpy as jnp
from jax import lax
from jax.experimental import pallas as pl
from jax.experimental.pallas import tpu as pltpu

F32 = jnp.float32
BF16 = jnp.bfloat16

D_MODEL = 1024
DEPTH = 4
MIX_W = 512
N_BRANCH = 4
N_HEAD = 4
D_HEAD = 128
GRID_W = 64
LRU_C = 8.0
CONV_K = 4
CONV_LEFT = 2
CHUNK = 64
RET_CHUNK = 256
SUB = 16
GROUP = 16
GROUP_D = 8
LRU_UNROLL = 4
PACK = 4
ROPE_BASE = 10000.0
EPS = 1e-6
LANES = 128
SUBLANES = 8
VMEM_LIMIT = 56 * 1024 * 1024

CB_AQ, CB_AK, CB_AV, CB_BX, CB_CQ, CB_CK, CB_CV, CB_DQ, CB_DF, CB_DI, CB_CAB = (
    0, 4, 8, 12, 16, 20, 24, 28, 32, 40, 44)
MIX_COLS = 45 * LANES
TN_IN = 1920


def _dot(a, b):
    return jnp.dot(a, b, preferred_element_type=F32)


def _dot_nt(a, b):
    return lax.dot_general(a, b, (((1,), (1,)), ((), ())), preferred_element_type=F32)


def _dot_tn(a, b):
    return lax.dot_general(a, b, (((0,), (0,)), ((), ())), preferred_element_type=F32)


def _split2(x):
    hi = x.astype(BF16)
    lo = (x - hi.astype(F32)).astype(BF16)
    return hi, lo


def _dot_split(a, b):
    return _dot(a[0], b[0]) + (_dot(a[0], b[1]) + _dot(a[1], b[0]))


def _dot_mask(mask_bf16, x):
    hi = x.astype(BF16)
    r1 = x - hi.astype(F32)
    mid = r1.astype(BF16)
    lo = (r1 - mid.astype(F32)).astype(BF16)
    return _dot(mask_bf16, hi) + (_dot(mask_bf16, mid) + _dot(mask_bf16, lo))


def _order_mask(n, reverse, strict):
    i = lax.broadcasted_iota(jnp.int32, (n, n), 0)
    j = lax.broadcasted_iota(jnp.int32, (n, n), 1)
    if reverse:
        return (j > i) if strict else (j >= i)
    return (j < i) if strict else (j <= i)


def _rows(cc, size=CHUNK):
    return pl.ds(pl.multiple_of(cc * size, size), size)


def _conv_chunk(x_ref, cc, nc, n_rows, w):
    r0 = pl.multiple_of(cc * CHUNK, CHUNK)
    cur = x_ref[pl.ds(r0, CHUNK), :]
    p0 = pl.multiple_of(jnp.maximum(r0 - SUBLANES, 0), SUBLANES)
    n0 = pl.multiple_of(jnp.minimum(r0 + CHUNK, n_rows - SUBLANES), SUBLANES)
    in_seq = cc % nc
    prev = x_ref[pl.ds(p0, SUBLANES), :] * jnp.where(in_seq > 0, 1.0, 0.0)
    nxt = x_ref[pl.ds(n0, SUBLANES), :] * jnp.where(in_seq < nc - 1, 1.0, 0.0)
    xw = jnp.concatenate([prev, cur, nxt], axis=0)
    rows = CHUNK + 2 * SUBLANES
    acc = None
    for j in range(CONV_K):
        shift = (CONV_LEFT - j) % rows
        rolled = xw if shift == 0 else pltpu.roll(xw, shift, 0)
        term = rolled[SUBLANES:SUBLANES + CHUNK, :] * w[j:j + 1, :]
        acc = term if acc is None else acc + term
    return acc


def _silu(x):
    return x * jax.nn.sigmoid(x)


def _rms(x):
    return x * lax.rsqrt(jnp.mean(x * x, axis=-1, keepdims=True) + EPS)


def _mod_kernel(m_ref, w_ref, b_ref, o_ref):
    m = _silu(m_ref[...])
    o_ref[...] = _dot(m.astype(BF16), w_ref[...].astype(BF16)) + b_ref[...]


def _mod_vectors(mvecs, w_mod, b_mod):
    tn = D_MODEL
    return pl.pallas_call(
        _mod_kernel,
        grid=(DEPTH, 3 * D_MODEL // tn),
        in_specs=[
            pl.BlockSpec((SUBLANES, D_MODEL), lambda l, j: (0, 0)),
            pl.BlockSpec((None, D_MODEL, tn), lambda l, j: (l, 0, j)),
            pl.BlockSpec((None, 1, tn), lambda l, j: (l, 0, j)),
        ],
        out_specs=pl.BlockSpec((None, SUBLANES, tn), lambda l, j: (l, 0, j)),
        out_shape=jax.ShapeDtypeStruct((DEPTH, SUBLANES, 3 * D_MODEL), F32),
        compiler_params=pltpu.CompilerParams(vmem_limit_bytes=VMEM_LIMIT),
        name="mod_vectors",
    )(mvecs, w_mod, b_mod.reshape(DEPTH, 1, 3 * D_MODEL))


def _modulated_norm(x, g, mod):
    return _rms(x) * g * (1.0 + mod[1:2, :]) + mod[0:1, :]


def _inproj_kernel(x_ref, mod_ref, g_ref, w_ref, o_ref, h_ref):
    @pl.when(pl.program_id(1) == 0)
    def _():
        h_ref[...] = _modulated_norm(x_ref[...], g_ref[...], mod_ref[...]).astype(BF16)

    o_ref[...] = _dot(h_ref[...], w_ref[...])


def _in_projection(x, mods, mod_row, g, w1, layer, tm):
    t = x.shape[0]
    return pl.pallas_call(
        _inproj_kernel,
        grid=(t // tm, MIX_COLS // TN_IN),
        in_specs=[
            pl.BlockSpec((tm, D_MODEL), lambda i, j: (i, 0)),
            pl.BlockSpec((None, 3, D_MODEL), lambda i, j: (mod_row(i), 0, 0)),
            pl.BlockSpec((1, D_MODEL), lambda i, j: (0, 0)),
            pl.BlockSpec((None, D_MODEL, TN_IN), lambda i, j: (layer, 0, j)),
        ],
        out_specs=pl.BlockSpec((tm, TN_IN), lambda i, j: (i, j)),
        out_shape=jax.ShapeDtypeStruct((t, MIX_COLS), F32),
        scratch_shapes=[pltpu.VMEM((tm, D_MODEL), BF16)],
        compiler_params=pltpu.CompilerParams(
            dimension_semantics=("parallel", "arbitrary"), vmem_limit_bytes=VMEM_LIMIT),
        name="in_projection",
    )(x, mods, g, w1)


def _merge_kernel(x_ref, mod_ref, g_ref, oa_ref, ob_ref, oc_ref, od_ref, wz_ref, wg_ref,
                  wb_ref, wo_ref, *rest, final):
    if final:
        fg_ref, o_ref, y_ref = rest
    else:
        (o_ref,) = rest
    x = x_ref[...]
    mod = mod_ref[...]
    h = _modulated_norm(x, g_ref[...], mod).astype(BF16)
    acc = None
    for k, b_ref in enumerate((oa_ref, ob_ref, oc_ref, od_ref)):
        z = _dot(h, wz_ref[:, k * MIX_W:(k + 1) * MIX_W])
        br = (b_ref[...] * _silu(z)).astype(BF16)
        yb = _dot(br, wb_ref[k])
        term = jax.nn.sigmoid(_dot(h, wg_ref[:, k * D_MODEL:(k + 1) * D_MODEL])) * yb
        acc = term if acc is None else acc + term
    y = _dot(acc.astype(BF16), wo_ref[...])
    out = x + mod[2:3, :] * y
    o_ref[...] = out
    if final:
        y_ref[...] = _rms(out) * fg_ref[...]


def _merge(x, mods, mod_row, g, outs, wz, wg, wb, wo, layer, final_g, tm):
    t = x.shape[0]
    final = final_g is not None
    const = dict(pipeline_mode=pl.Buffered(1))
    in_specs = [
        pl.BlockSpec((tm, D_MODEL), lambda i: (i, 0)),
        pl.BlockSpec((None, 3, D_MODEL), lambda i: (mod_row(i), 0, 0)),
        pl.BlockSpec((1, D_MODEL), lambda i: (0, 0)),
    ] + [pl.BlockSpec((tm, MIX_W), lambda i: (i, 0)) for _ in range(N_BRANCH)] + [
        pl.BlockSpec((None, D_MODEL, N_BRANCH * MIX_W), lambda i: (layer, 0, 0), **const),
        pl.BlockSpec((None, D_MODEL, N_BRANCH * D_MODEL), lambda i: (layer, 0, 0), **const),
        pl.BlockSpec((None, N_BRANCH, MIX_W, D_MODEL), lambda i: (layer, 0, 0, 0), **const),
        pl.BlockSpec((None, D_MODEL, D_MODEL), lambda i: (layer, 0, 0), **const),
    ]
    args = [x, mods, g, *outs, wz, wg, wb, wo]
    out_specs = [pl.BlockSpec((tm, D_MODEL), lambda i: (i, 0))]
    out_shape = [jax.ShapeDtypeStruct((t, D_MODEL), F32)]
    if final:
        in_specs.append(pl.BlockSpec((1, D_MODEL), lambda i: (0, 0)))
        args.append(final_g)
        out_specs.append(pl.BlockSpec((tm, D_MODEL), lambda i: (i, 0)))
        out_shape.append(jax.ShapeDtypeStruct((t, D_MODEL), F32))
    res = pl.pallas_call(
        functools.partial(_merge_kernel, final=final),
        grid=(t // tm,),
        in_specs=in_specs,
        out_specs=out_specs,
        out_shape=out_shape,
        compiler_params=pltpu.CompilerParams(
            dimension_semantics=("parallel",), vmem_limit_bytes=VMEM_LIMIT),
        name="merge",
    )(*args)
    return res if final else (res[0], None)


def _state_specs(layer):
    s_in = pl.BlockSpec((None, None, 2, None, D_HEAD, D_HEAD), lambda b, h: (b, layer, 0, h, 0, 0))
    s_out = pl.BlockSpec((None, 2, None, D_HEAD, D_HEAD), lambda b, h: (b, 0, h, 0, 0))
    return s_in, s_out


def _col_spec(n_tok, cb):
    return pl.BlockSpec((n_tok, LANES), lambda b, h: (b, cb + h))


def _head_mixer_call(kernel, name, bsz, n_tok, in_specs, args, scratch=(), seqs=None):
    _, s_out = _state_specs(0)
    if seqs is not None:
        s_out = pl.BlockSpec((seqs, 2, None, D_HEAD, D_HEAD), lambda b, h: (b, 0, h, 0, 0))
    per_step = seqs or 1
    return pl.pallas_call(
        kernel,
        grid=(bsz // per_step, N_HEAD),
        in_specs=in_specs,
        out_specs=[pl.BlockSpec((per_step * n_tok, LANES), lambda b, h: (b, h)), s_out],
        out_shape=[jax.ShapeDtypeStruct((bsz * n_tok, MIX_W), F32),
                   jax.ShapeDtypeStruct((bsz, 2, N_HEAD, D_HEAD, D_HEAD), F32)],
        scratch_shapes=list(scratch),
        compiler_params=pltpu.CompilerParams(
            dimension_semantics=("parallel", "arbitrary"), vmem_limit_bytes=VMEM_LIMIT),
        name=name,
    )(*args)


def _zero_rows(o_ref, n_tok):
    def body(c, carry):
        o_ref[_rows(c), :] = jnp.zeros((CHUNK, LANES), F32)
        return carry

    lax.fori_loop(0, n_tok // CHUNK, body, 0)


def _norm_rows(o_ref, n_tok, g_ref):
    def body(c, carry):
        o_ref[_rows(c), :] = _rms(o_ref[_rows(c), :]) * g_ref[...]
        return carry

    lax.fori_loop(0, n_tok // CHUNK, body, 0)


def _ret_kernel(lg_ref, q_ref, k_ref, v_ref, *rest, n_tok, rope, has_init):
    rest = list(rest)
    cos_ref, sin_ref = (rest.pop(0), rest.pop(0)) if rope else (None, None)
    s0_ref = rest.pop(0) if has_init else None
    o_ref, sf_ref, kvb_ref = rest
    h = pl.program_id(1)
    size = RET_CHUNK
    nc = n_tok // size
    lg_f, lg_b = lg_ref[0, h], lg_ref[1, h]
    pos_c = lax.broadcasted_iota(jnp.int32, (size, 1), 0).astype(F32)
    pos_r = lax.broadcasted_iota(jnp.int32, (1, size), 1).astype(F32)
    rel = pos_c - pos_r
    lower, upper = rel >= 0, rel <= 0
    dsum = (jnp.where(lower, jnp.exp(jnp.where(lower, rel, 0.0) * lg_f), 0.0)
            + jnp.where(upper, jnp.exp(jnp.where(upper, -rel, 0.0) * lg_b), 0.0))
    q_dec_f = jnp.exp((pos_c + 1.0) * lg_f)
    k_dec_f = jnp.exp((size - 1.0 - pos_c) * lg_f)
    q_dec_b = jnp.exp((size - pos_c) * lg_b)
    k_dec_b = jnp.exp(pos_c * lg_b)
    c_dec_f = jnp.exp(jnp.full((1, D_HEAD), size * lg_f, F32))
    c_dec_b = jnp.exp(jnp.full((1, D_HEAD), size * lg_b, F32))

    def rotate(x, rows):
        if not rope:
            return x
        return x * cos_ref[rows, :] + pltpu.roll(x, D_HEAD // 2, 1) * sin_ref[rows, :]

    unroll = 2 if nc % 2 == 0 else 1

    def forward(i, s):
        cs = [i * unroll + j for j in range(unroll)]
        rows = [_rows(c, size) for c in cs]
        q = [rotate(q_ref[r, :], r) for r in rows]
        k = [rotate(k_ref[r, :], r) * D_HEAD ** -0.5 for r in rows]
        v = [v_ref[r, :] for r in rows]
        scores = [_dot_nt(qj, kj) for qj, kj in zip(q, k)]
        kv_b = [_dot_tn(kj * k_dec_b, vj) for kj, vj in zip(k, v)]
        kv_f = [_dot_tn(kj * k_dec_f, vj) for kj, vj in zip(k, v)]
        inter = []
        for j in range(unroll):
            inter.append(_dot(q[j] * q_dec_f, s))
            s = s * c_dec_f + kv_f[j]
        for j in range(unroll):
            o_ref[rows[j], :] = _dot(scores[j] * dsum, v[j]) + inter[j]
            kvb_ref[cs[j]] = kv_b[j]
        return s

    s0 = s0_ref[0] if has_init else jnp.zeros((D_HEAD, D_HEAD), F32)
    sf_ref[0] = lax.fori_loop(0, nc // unroll, forward, s0)

    def backward(i, s):
        for j in range(unroll):
            c = nc - 1 - (i * unroll + j)
            rows = _rows(c, size)
            q = rotate(q_ref[rows, :], rows)
            o_ref[rows, :] = _rms(o_ref[rows, :] + _dot(q * q_dec_b, s))
            s = s * c_dec_b + kvb_ref[c]
        return s

    s0 = s0_ref[1] if has_init else jnp.zeros((D_HEAD, D_HEAD), F32)
    sf_ref[1] = lax.fori_loop(0, nc // unroll, backward, s0)


def _retention(proj, log_gamma, rope, state, layer, bsz, n_tok):
    in_specs = [pl.BlockSpec(memory_space=pltpu.SMEM),
                _col_spec(n_tok, CB_AQ), _col_spec(n_tok, CB_AK), _col_spec(n_tok, CB_AV)]
    args = [log_gamma, proj, proj, proj]
    if rope is not None:
        in_specs += [pl.BlockSpec((n_tok, LANES), lambda b, h: (0, 0))] * 2
        args += list(rope)
    if state is not None:
        in_specs.append(_state_specs(layer)[0])
        args.append(state)
    kernel = functools.partial(_ret_kernel, n_tok=n_tok, rope=rope is not None,
                               has_init=state is not None)
    scratch = [pltpu.VMEM((n_tok // RET_CHUNK, D_HEAD, D_HEAD), F32)]
    return _head_mixer_call(kernel, "retention", bsz, n_tok, in_specs, args, scratch)


def _tile_scan(a, u, reverse):
    row = lax.broadcasted_iota(jnp.int32, a.shape, 0)
    for s in (1, 2, 4):
        if reverse:
            valid = row < SUBLANES - s
            shift = SUBLANES - s
        else:
            valid = row >= s
            shift = s
        a_sh = jnp.where(valid, pltpu.roll(a, shift, 0), 1.0)
        u_sh = jnp.where(valid, pltpu.roll(u, shift, 0), 0.0)
        u = a * u_sh + u
        a = a * a_sh
    return a, u


def _lru_kernel(x_ref, cw_ref, cb_ref, gw_ref, gb_ref, lam_ref, *rest, n_tok, has_init):
    rest = list(rest)
    h0_ref = rest.pop(0) if has_init else None
    o_ref, hl_ref = rest
    nc = n_tok // CHUNK
    n_tile = CHUNK // SUBLANES
    cw = cw_ref[...]
    cb = cb_ref[...]
    _zero_rows(o_ref, n_tok)

    unroll = min(LRU_UNROLL, nc)
    coef = [-LRU_C * jax.nn.softplus(-lam_ref[d]) for d in range(2)]

    def body(i, carry):
        ctx = [(d, (nc - 1 - (i * unroll + j)) if d == 1 else i * unroll + j)
               for j in range(unroll) for d in range(2)]
        xb = [_conv_chunk(x_ref, cc, nc, n_tok, cw) + cb for _, cc in ctx]
        pre = [(_dot(x, gw_ref[d, 0]), _dot(x, gw_ref[d, 1])) for x, (d, _) in zip(xb, ctx)]
        carry = list(carry)
        for m, (d, cc) in enumerate(ctx):
            r = jax.nn.sigmoid(pre[m][0] + gb_ref[2 * d])
            gate = jax.nn.sigmoid(pre[m][1] + gb_ref[2 * d + 1])
            log_a = coef[d] * r
            a = jnp.exp(log_a)
            one_minus_a2 = -jnp.tanh(log_a) * (a * a + 1.0)
            u = jnp.sqrt(jnp.maximum(one_minus_a2, 1e-12)) * (gate * xb[m])
            scans = [_tile_scan(a[t * SUBLANES:(t + 1) * SUBLANES, :], u[t * SUBLANES:(t + 1) * SUBLANES, :],
                                d == 1) for t in range(n_tile)]
            tiles = [None] * n_tile
            last = 0 if d == 1 else SUBLANES - 1
            for t in (range(n_tile - 1, -1, -1) if d == 1 else range(n_tile)):
                a_t, u_t = scans[t]
                tiles[t] = a_t * carry[d] + u_t
                carry[d] = a_t[last:last + 1, :] * carry[d] + u_t[last:last + 1, :]
            rows = _rows(cc)
            o_ref[rows, :] = o_ref[rows, :] + jnp.concatenate(tiles, axis=0)
        return tuple(carry)

    zero = jnp.zeros((1, LANES), F32)
    h0 = (h0_ref[0], h0_ref[1]) if has_init else (zero, zero)
    h_f, h_b = lax.fori_loop(0, nc // unroll, body, h0)
    hl_ref[0] = h_f
    hl_ref[1] = h_b


def _rglru(proj, conv_w, conv_b, gate_w, gate_b, lam, state, layer, bsz, n_tok):
    n_pair = MIX_W // LANES
    in_specs = [
        pl.BlockSpec((n_tok, LANES), lambda b, p: (b, CB_BX + p)),
        pl.BlockSpec((CONV_K, LANES), lambda b, p: (0, p)),
        pl.BlockSpec((1, LANES), lambda b, p: (0, p)),
        pl.BlockSpec((2, 2, None, LANES, LANES), lambda b, p: (0, 0, p, 0, 0)),
        pl.BlockSpec((4, 1, LANES), lambda b, p: (0, 0, p)),
        pl.BlockSpec((2, 1, LANES), lambda b, p: (0, 0, p)),
    ]
    args = [proj, conv_w, conv_b, gate_w, gate_b, lam]
    if state is not None:
        in_specs.append(pl.BlockSpec((None, None, 2, 1, LANES), lambda b, p: (b, layer, 0, 0, p)))
        args.append(state)
    return pl.pallas_call(
        functools.partial(_lru_kernel, n_tok=n_tok, has_init=state is not None),
        grid=(bsz, n_pair),
        in_specs=in_specs,
        out_specs=[pl.BlockSpec((n_tok, LANES), lambda b, p: (b, p)),
                   pl.BlockSpec((None, 2, 1, LANES), lambda b, p: (b, 0, 0, p))],
        out_shape=[jax.ShapeDtypeStruct((bsz * n_tok, MIX_W), F32),
                   jax.ShapeDtypeStruct((bsz, 2, 1, MIX_W), F32)],
        compiler_params=pltpu.CompilerParams(
            dimension_semantics=("parallel", "arbitrary"), vmem_limit_bytes=VMEM_LIMIT),
        name="rglru",
    )(*args)


def _unit_lower_inverses(mats):
    wide = PACK * CHUNK
    i = lax.broadcasted_iota(jnp.int32, (CHUNK, wide), 0)
    j = lax.broadcasted_iota(jnp.int32, (CHUNK, wide), 1)
    eye = jnp.where(i == j % CHUNK, 1.0, 0.0)
    col_blk = j // CHUNK

    def block_diag(x):
        return jnp.concatenate([jnp.where(col_blk == b, x, jnp.zeros_like(x)) for b in range(PACK)], axis=0)

    steps = CHUNK.bit_length() - 2
    ps = [-jnp.concatenate(mats[g:g + PACK], axis=1) for g in range(0, len(mats), PACK)]
    ts = [eye + p for p in ps]
    halves = [_split2(p) for p in ps]
    diags = [(block_diag(hi), block_diag(lo)) for hi, lo in halves]
    for _ in range(steps):
        ps = [_dot_split(hp, dg) for hp, dg in zip(halves, diags)]
        halves = [_split2(p) for p in ps]
        diags = [(block_diag(hi), block_diag(lo)) for hi, lo in halves]
        ts = [t + _dot_split(_split2(t), dg) for t, dg in zip(ts, diags)]
    return [t[:, b * CHUNK:(b + 1) * CHUNK] for t in ts for b in range(PACK)]


def _gdn_kernel(gp_ref, q_ref, k_ref, v_ref, ab_ref, wq_ref, wk_ref, wv_ref, g_ref,
                *rest, n_tok, seqs, has_init):
    rest = list(rest)
    s0_ref = rest.pop(0) if has_init else None
    o_ref, sf_ref, qt_ref, kw_ref, bm_ref, egl_ref = rest
    h = pl.program_id(1)
    nc = n_tok // CHUNK
    n_rows = seqs * n_tok
    n_chunks = seqs * nc
    group = min(GROUP, n_chunks)
    lane = lax.broadcasted_iota(jnp.int32, (CHUNK, LANES), 1)
    masks = {d: (_order_mask(CHUNK, d == 1, strict=False), _order_mask(CHUNK, d == 1, strict=True))
             for d in range(2)}

    def front(i, carry):
        chunks = [i * group + j for j in range(group)]
        rows = [_rows(c) for c in chunks]
        q, k, v, ab = [], [], [], []
        for c, r in zip(chunks, rows):
            qc = _silu(_conv_chunk(q_ref, c, nc, n_rows, wq_ref[...]))
            kc = _silu(_conv_chunk(k_ref, c, nc, n_rows, wk_ref[...]))
            q.append(qc * (lax.rsqrt(jnp.sum(qc * qc, axis=-1, keepdims=True) + EPS) * D_HEAD ** -0.5))
            k.append(kc * lax.rsqrt(jnp.sum(kc * kc, axis=-1, keepdims=True) + EPS))
            v.append(_silu(_conv_chunk(v_ref, c, nc, n_rows, wv_ref[...])))
            x_ab = ab_ref[r, :]
            ab.append((gp_ref[0:1, :] * jax.nn.softplus(x_ab + gp_ref[1:2, :]), jax.nn.sigmoid(x_ab)))
        ctx = [(j, d) for j in range(group) for d in range(2)]
        beta, both = [], []
        for j, d in ctx:
            incl, strict = masks[d]
            g = jnp.sum(jnp.where(lane == d * N_HEAD + h, ab[j][0], 0.0), axis=1, keepdims=True)
            beta.append(jnp.sum(jnp.where(lane == (2 + d) * N_HEAD + h, ab[j][1], 0.0), axis=1, keepdims=True))
            rhs = jnp.concatenate([jnp.broadcast_to(g, (CHUNK, CHUNK)), g * strict.astype(F32)], axis=1)
            both.append(_dot_mask(incl.astype(BF16), rhs))
        qk_kk = [_dot_nt(jnp.concatenate([q[j], k[j]], axis=0), k[j]) for j in range(group)]
        gc, gam, a_mat = [], [], []
        for m, (j, d) in enumerate(ctx):
            incl, strict = masks[d]
            gc.append(jnp.broadcast_to(both[m][:, 0:1], (CHUNK, LANES)))
            gam.append(jnp.where(incl, jnp.exp(jnp.where(incl, both[m][:, CHUNK:], 0.0)), 0.0))
            a_mat.append(jnp.where(strict, beta[m] * qk_kk[j][CHUNK:] * gam[m], 0.0))
        t_inv = _unit_lower_inverses(a_mat)
        e_gc = [jnp.exp(x) for x in gc]
        wu = [_dot(t_inv[m], jnp.concatenate([k[j] * (beta[m] * e_gc[m]), v[j] * beta[m]], axis=1))
              for m, (j, d) in enumerate(ctx)]
        a_wu = [_dot(qk_kk[j][:CHUNK] * gam[m], wu[m]) for m, (j, d) in enumerate(ctx)]
        gl = [gc[m][(0 if d == 1 else CHUNK - 1):(1 if d == 1 else CHUNK), :] for m, (j, d) in enumerate(ctx)]
        kd_wu = [_dot_tn(k[j] * jnp.exp(gl[m] - gc[m]), wu[m]) for m, (j, d) in enumerate(ctx)]
        for m, (j, d) in enumerate(ctx):
            qt_ref[d, rows[j], :] = (q[j] * e_gc[m] - a_wu[m][:, :D_HEAD]).astype(BF16)
            kw_ref[d, chunks[j]] = kd_wu[m][:, :D_HEAD].astype(BF16)
            bm_ref[d, chunks[j]] = kd_wu[m][:, D_HEAD:]
            egl_ref[d, chunks[j]] = jnp.broadcast_to(jnp.exp(gl[m]), (SUBLANES, LANES))
        for j in range(group):
            o_ref[rows[j], :] = a_wu[2 * j][:, D_HEAD:] + a_wu[2 * j + 1][:, D_HEAD:]
        return carry

    lax.fori_loop(0, n_chunks // group, front, 0)

    unroll = 2 if seqs == 1 else 1
    chains = [(sq, d) for sq in range(seqs) for d in range(2)]

    def back(i, carry):
        s = list(carry)
        for u in range(unroll):
            step = i * unroll + u
            for n, (sq, d) in enumerate(chains):
                c = sq * nc + ((nc - 1 - step) if d == 1 else step)
                rows = _rows(c)
                s_bf = s[n].astype(BF16)
                o_ref[rows, :] = o_ref[rows, :] + _dot(qt_ref[d, rows, :], s_bf)
                s[n] = s[n] * egl_ref[d, c][0:1, :] - _dot(kw_ref[d, c], s_bf) + bm_ref[d, c]
        return tuple(s)

    zero = jnp.zeros((D_HEAD, D_HEAD), F32)
    s0 = tuple(s0_ref[sq, d] if has_init else zero for sq, d in chains)
    s_fin = lax.fori_loop(0, nc // unroll, back, s0)
    for n, (sq, d) in enumerate(chains):
        sf_ref[sq, d] = s_fin[n]
    _norm_rows(o_ref, n_rows, g_ref)


def _gated_delta(proj, neg_a, dt_bias, conv_w, norm_g, state, layer, bsz, n_tok):
    seqs = max(1, min(bsz, GROUP * CHUNK // n_tok))
    rows = seqs * n_tok

    def col_spec(cb):
        return pl.BlockSpec((rows, LANES), lambda b, h: (b, cb + h))

    in_specs = [pl.BlockSpec((2, LANES), lambda b, h: (0, 0)),
                col_spec(CB_CQ), col_spec(CB_CK), col_spec(CB_CV),
                pl.BlockSpec((rows, LANES), lambda b, h: (b, CB_CAB)),
                pl.BlockSpec((CONV_K, LANES), lambda b, h: (0, h)),
                pl.BlockSpec((CONV_K, LANES), lambda b, h: (0, N_HEAD + h)),
                pl.BlockSpec((CONV_K, LANES), lambda b, h: (0, 2 * N_HEAD + h)),
                pl.BlockSpec((1, D_HEAD), lambda b, h: (0, 0))]
    gate_par = jnp.zeros((2, LANES), F32).at[0, :2 * N_HEAD].set(neg_a.reshape(-1)).at[1, :2 * N_HEAD].set(
        dt_bias.reshape(-1))
    args = [gate_par, proj, proj, proj, proj, conv_w, conv_w, conv_w, norm_g]
    if state is not None:
        in_specs.append(pl.BlockSpec((seqs, None, 2, None, D_HEAD, D_HEAD),
                                     lambda b, h: (b, layer, 0, h, 0, 0)))
        args.append(state)
    kernel = functools.partial(_gdn_kernel, n_tok=n_tok, seqs=seqs, has_init=state is not None)
    n_chunks = rows // CHUNK
    scratch = [pltpu.VMEM((2, rows, D_HEAD), BF16),
               pltpu.VMEM((2, n_chunks, D_HEAD, D_HEAD), BF16),
               pltpu.VMEM((2, n_chunks, D_HEAD, D_HEAD), F32),
               pltpu.VMEM((2, n_chunks, SUBLANES, LANES), F32)]
    return _head_mixer_call(kernel, "gated_delta", bsz, n_tok, in_specs, args, scratch, seqs=seqs)


def _block_rows(x, row):
    nb = CHUNK // SUB
    x3 = x.reshape(nb, SUB, LANES)[:, row:row + 1, :]
    return jnp.broadcast_to(x3, (nb, SUB, LANES)).reshape(CHUNK, LANES)


def _hgrn_kernel(q_ref, f0_ref, f1_ref, v_ref, lb_ref, g_ref, *rest, n_tok, seqs, has_init):
    rest = list(rest)
    s0_ref = rest.pop(0) if has_init else None
    o_ref, sf_ref = rest
    nc = n_tok // CHUNK
    nb = CHUNK // SUB
    blk_c = lax.broadcasted_iota(jnp.int32, (CHUNK, 1), 0) // SUB
    blk_i = lax.broadcasted_iota(jnp.int32, (CHUNK, CHUNK), 0) // SUB
    blk_j = lax.broadcasted_iota(jnp.int32, (CHUNK, CHUNK), 1) // SUB
    _zero_rows(o_ref, seqs * n_tok)
    per = max(1, GROUP_D // seqs)

    def body(i, carry):
        ctx = [(2 * sq + d, sq * nc + ((nc - 1 - (i * per + j)) if d == 1 else i * per + j))
               for j in range(per) for sq in range(seqs) for d in range(2)]
        n = len(ctx)
        dirs = [chain % 2 for chain, _ in ctx]
        incl = {d: _order_mask(CHUNK, d == 1, strict=False) for d in range(2)}
        rows = [_rows(cc) for _, cc in ctx]
        q = [_silu(q_ref[r, :]) * D_HEAD ** -0.5 for r in rows]
        v = [v_ref[r, :] for r in rows]
        k, log_f, bc = [], [], []
        for d, r in zip(dirs, rows):
            zf = (f1_ref if d == 1 else f0_ref)[r, :]
            lb = lb_ref[d:d + 1, :]
            k.append((1.0 - lb) * jax.nn.sigmoid(-zf))
            f_gate = lb + (1.0 - lb) * jax.nn.sigmoid(zf)
            log_f.append(jnp.log(jnp.maximum(f_gate, 1e-30)))
            bc.append(_dot_mask(incl[d].astype(BF16), log_f[-1]))
        ex = [b - lf for b, lf in zip(bc, log_f)]
        a_diag = []
        for m, d in enumerate(dirs):
            m_blk = _block_rows(bc[m], SUB // 2 if d == 1 else SUB // 2 - 1)
            a_diag.append(_dot_nt(q[m] * jnp.exp(bc[m] - m_blk), k[m] * jnp.exp(m_blk - bc[m])))
        a_mat = []
        for m, d in enumerate(dirs):
            first = SUB - 1 if d == 1 else 0
            q_c = q[m] * jnp.exp(bc[m] - _block_rows(ex[m], first))
            cross = []
            for blk in range(nb):
                if blk == (nb - 1 if d == 1 else 0):
                    cross.append(jnp.zeros((SUB, CHUNK), F32))
                    continue
                earlier = (blk_c > blk) if d == 1 else (blk_c < blk)
                r_i = ex[m][blk * SUB + first:blk * SUB + first + 1, :]
                k_t = jnp.where(earlier, k[m] * jnp.exp(jnp.where(earlier, r_i - bc[m], 0.0)), 0.0)
                cross.append(_dot_nt(q_c[blk * SUB:(blk + 1) * SUB, :], k_t))
            diag = incl[d] & (blk_i == blk_j)
            a_mat.append(jnp.where(diag, a_diag[m], 0.0) + jnp.concatenate(cross, axis=0))
        b_last = [bc[m][(0 if d == 1 else CHUNK - 1):(1 if d == 1 else CHUNK), :] for m, d in enumerate(dirs)]
        kv = [_dot_tn(v[m], k[m] * jnp.exp(b_last[m] - bc[m])) for m in range(n)]
        intra = [_dot(a_mat[m], v[m]) for m in range(n)]
        st = list(carry)
        for m, (chain, _) in enumerate(ctx):
            o = intra[m] + _dot_nt(q[m] * jnp.exp(bc[m]), st[chain])
            o_ref[rows[m], :] = o_ref[rows[m], :] + o
            st[chain] = st[chain] * jnp.exp(b_last[m]) + kv[m]
        return tuple(st)

    zero = jnp.zeros((D_HEAD, D_HEAD), F32)
    st0 = tuple(s0_ref[sq, d].T if has_init else zero for sq in range(seqs) for d in range(2))
    st_fin = lax.fori_loop(0, nc // per, body, st0)
    for sq in range(seqs):
        for d in range(2):
            sf_ref[sq, d] = st_fin[2 * sq + d].T
    _norm_rows(o_ref, seqs * n_tok, g_ref)


def _hgrn2(proj, lower_bound, norm_g, state, layer, bsz, n_tok):
    seqs = max(1, min(bsz, GROUP_D * CHUNK // n_tok))
    rows = seqs * n_tok
    in_specs = [_col_spec(rows, CB_DQ), _col_spec(rows, CB_DF), _col_spec(rows, CB_DF + N_HEAD),
                _col_spec(rows, CB_DI),
                pl.BlockSpec((2, D_HEAD), lambda b, h: (0, h)),
                pl.BlockSpec((1, D_HEAD), lambda b, h: (0, 0))]
    args = [proj, proj, proj, proj, lower_bound, norm_g]
    if state is not None:
        in_specs.append(pl.BlockSpec((seqs, None, 2, None, D_HEAD, D_HEAD),
                                     lambda b, h: (b, layer, 0, h, 0, 0)))
        args.append(state)
    kernel = functools.partial(_hgrn_kernel, n_tok=n_tok, seqs=seqs, has_init=state is not None)
    return _head_mixer_call(kernel, "hgrn2", bsz, n_tok, in_specs, args, seqs=seqs)


def _rope_tables(n_tok):
    n_freq = D_HEAD // 4
    inv = ROPE_BASE ** (-jnp.arange(n_freq, dtype=F32) / n_freq)
    rows = n_tok // GRID_W
    r = jnp.repeat(jnp.arange(rows, dtype=F32), GRID_W)
    col = jnp.tile(jnp.arange(GRID_W, dtype=F32), rows)
    ang = jnp.concatenate([r[:, None] * inv, col[:, None] * inv], axis=-1)
    cos, sin = jnp.cos(ang), jnp.sin(ang)
    return jnp.concatenate([cos, cos], axis=-1), jnp.concatenate([-sin, sin], axis=-1)


def _split_w_in(w):
    m = MIX_W
    a = w[..., 0:4 * m]
    b = w[..., 4 * m:6 * m]
    c = w[..., 6 * m:10 * m + 16]
    dd = w[..., 10 * m + 16:15 * m + 16]
    mg = w[..., 15 * m + 16:]
    cab = jnp.pad(c[..., 4 * m:], ((0, 0), (0, 0), (0, LANES - 16)))
    w1 = jnp.concatenate([a[..., :3 * m], b[..., :m], c[..., :3 * m], dd[..., :4 * m], cab], axis=-1)
    wz = jnp.concatenate([a[..., 3 * m:], b[..., m:], c[..., 3 * m:4 * m], dd[..., 4 * m:]], axis=-1)
    return w1, wz, mg


def _pair_block_diag(gw):
    z = jnp.zeros_like(gw[..., 0::2, :, :])
    top = jnp.concatenate([gw[..., 0::2, :, :], z], axis=-1)
    bot = jnp.concatenate([z, gw[..., 1::2, :, :]], axis=-1)
    return jnp.concatenate([top, bot], axis=-2)


def kernel(x_prompt, x_sample, state_ret, state_lru, state_gdn, state_hgrn, c, c_ctx, norm_g,
           w_mod, b_mod, w_in, ret_decay, lru_conv_w, lru_conv_b, lru_gate_w, lru_gate_b,
           lru_lambda, gdn_conv_w, gdn_a_log, gdn_dt_bias, gdn_norm_g, hgrn_lb, hgrn_norm_g,
           w_branch, w_out, final_norm_g):
    bp, n_ctx, _ = x_prompt.shape
    bs, n_lat, _ = x_sample.shape

    lb_sm = jax.nn.softmax(hgrn_lb.astype(F32), axis=0)
    lower_bounds = jnp.cumsum(lb_sm, axis=0) - lb_sm[0]
    log_gamma = jnp.log1p(-jnp.exp(ret_decay.astype(F32)))
    gdn_neg_a = -jnp.exp(gdn_a_log.astype(F32))
    rope = _rope_tables(n_lat)
    state_lru5 = state_lru.reshape(bs, DEPTH, 2, 1, MIX_W)
    w1, wz, wg = _split_w_in(w_in.astype(BF16))
    wb, wo = w_branch.astype(BF16), w_out.astype(BF16)
    gate_w = _pair_block_diag(lru_gate_w)

    mvecs = jnp.concatenate([c_ctx[None, :], c, jnp.zeros((SUBLANES - 1 - bs, D_MODEL), F32)], axis=0)
    mods = _mod_vectors(mvecs, w_mod, b_mod).reshape(DEPTH, SUBLANES, 3, D_MODEL)

    tm, tm_in = 512, 1024
    groups = [
        dict(x=x_prompt.reshape(bp * n_ctx, D_MODEL), bsz=bp, n_tok=n_ctx, rope=None, cached=False,
             mod_row=lambda i, t: 0),
        dict(x=x_sample.reshape(bs * n_lat, D_MODEL), bsz=bs, n_tok=n_lat, rope=rope, cached=True,
             mod_row=lambda i, t: 1 + (i * t) // n_lat),
    ]
    new_states = [[], [], [], []]
    finals = []
    for grp in groups:
        x, bsz, n_tok = grp["x"], grp["bsz"], grp["n_tok"]
        cached = grp["cached"]
        for l in range(DEPTH):
            g = norm_g[l][None, :]
            proj = _in_projection(x, mods[l], functools.partial(grp["mod_row"], t=tm_in), g, w1, l, tm_in)
            o_a, s_a = _retention(proj, log_gamma[l], grp["rope"], state_ret if cached else None,
                                  l, bsz, n_tok)
            gate_b = lru_gate_b[l].reshape(4, 1, MIX_W)
            o_b, s_b = _rglru(proj, lru_conv_w[l], lru_conv_b[l][None, :], gate_w[l], gate_b,
                              lru_lambda[l][:, None, :], state_lru5 if cached else None, l, bsz, n_tok)
            o_c, s_c = _gated_delta(proj, gdn_neg_a[l], gdn_dt_bias[l].astype(F32), gdn_conv_w[l],
                                    gdn_norm_g[l][None, :], state_gdn if cached else None, l, bsz, n_tok)
            o_d, s_d = _hgrn2(proj, lower_bounds[l], hgrn_norm_g[l][None, :],
                              state_hgrn if cached else None, l, bsz, n_tok)
            final_g = final_norm_g[None, :] if l == DEPTH - 1 else None
            x, y = _merge(x, mods[l], functools.partial(grp["mod_row"], t=tm), g, (o_a, o_b, o_c, o_d),
                          wz, wg, wb, wo, l, final_g, tm)
            if not cached:
                for acc, s in zip(new_states, (s_a, s_b, s_c, s_d)):
                    acc.append(s)
        finals.append(y)

    y_prompt = finals[0].reshape(bp, n_ctx, D_MODEL)
    y_sample = finals[1].reshape(bs, n_lat, D_MODEL)
    new_ret = jnp.stack(new_states[0], axis=1)
    new_lru = jnp.stack(new_states[1], axis=1).reshape(bp, DEPTH, 2, MIX_W)
    new_gdn = jnp.stack(new_states[2], axis=1)
    new_hgrn = jnp.stack(new_states[3], axis=1)
    return (y_prompt, y_sample, new_ret, new_lru, new_gdn, new_hgrn)
```

```python
import functools

import jax
import jax.numpy as jnp
from jax import lax
from jax.experimental import pallas as pl
from jax.experimental.pallas import tpu as pltpu

F32 = jnp.float32
BF16 = jnp.bfloat16

D_MODEL = 1024
DEPTH = 4
MIX_W = 512
N_BRANCH = 4
N_HEAD = 4
D_HEAD = 128
GRID_W = 64
LRU_C = 8.0
CONV_K = 4
CONV_LEFT = 2
CHUNK = 64
RET_CHUNK = 256
SUB = 16
GROUP = 16
GROUP_D = 8
LRU_UNROLL = 4
PACK = 4
ROPE_BASE = 10000.0
EPS = 1e-6
LANES = 128
SUBLANES = 8
VMEM_LIMIT = 56 * 1024 * 1024

CB_AQ, CB_AK, CB_AV, CB_BX, CB_CQ, CB_CK, CB_CV, CB_DQ, CB_DF, CB_DI, CB_CAB = (
    0, 4, 8, 12, 16, 20, 24, 28, 32, 40, 44)
MIX_COLS = 45 * LANES
TN_IN = 1920


def _dot(a, b):
    return jnp.dot(a, b, preferred_element_type=F32)


def _dot_nt(a, b):
    return lax.dot_general(a, b, (((1,), (1,)), ((), ())), preferred_element_type=F32)


def _dot_tn(a, b):
    return lax.dot_general(a, b, (((0,), (0,)), ((), ())), preferred_element_type=F32)


def _split2(x):
    hi = x.astype(BF16)
    lo = (x - hi.astype(F32)).astype(BF16)
    return hi, lo


def _dot_split(a, b):
    return _dot(a[0], b[0]) + (_dot(a[0], b[1]) + _dot(a[1], b[0]))


def _dot_mask(mask_bf16, x):
    hi = x.astype(BF16)
    r1 = x - hi.astype(F32)
    mid = r1.astype(BF16)
    lo = (r1 - mid.astype(F32)).astype(BF16)
    return _dot(mask_bf16, hi) + (_dot(mask_bf16, mid) + _dot(mask_bf16, lo))


def _order_mask(n, reverse, strict):
    i = lax.broadcasted_iota(jnp.int32, (n, n), 0)
    j = lax.broadcasted_iota(jnp.int32, (n, n), 1)
    if reverse:
        return (j > i) if strict else (j >= i)
    return (j < i) if strict else (j <= i)


def _rows(cc, size=CHUNK):
    return pl.ds(pl.multiple_of(cc * size, size), size)


def _conv_chunk(x_ref, cc, nc, n_rows, w):
    r0 = pl.multiple_of(cc * CHUNK, CHUNK)
    cur = x_ref[pl.ds(r0, CHUNK), :]
    p0 = pl.multiple_of(jnp.maximum(r0 - SUBLANES, 0), SUBLANES)
    n0 = pl.multiple_of(jnp.minimum(r0 + CHUNK, n_rows - SUBLANES), SUBLANES)
    in_seq = cc % nc
    prev = x_ref[pl.ds(p0, SUBLANES), :] * jnp.where(in_seq > 0, 1.0, 0.0)
    nxt = x_ref[pl.ds(n0, SUBLANES), :] * jnp.where(in_seq < nc - 1, 1.0, 0.0)
    xw = jnp.concatenate([prev, cur, nxt], axis=0)
    rows = CHUNK + 2 * SUBLANES
    acc = None
    for j in range(CONV_K):
        shift = (CONV_LEFT - j) % rows
        rolled = xw if shift == 0 else pltpu.roll(xw, shift, 0)
        term = rolled[SUBLANES:SUBLANES + CHUNK, :] * w[j:j + 1, :]
        acc = term if acc is None else acc + term
    return acc


def _silu(x):
    return x * jax.nn.sigmoid(x)


def _rms(x):
    return x * lax.rsqrt(jnp.mean(x * x, axis=-1, keepdims=True) + EPS)


def _mod_kernel(m_ref, w_ref, b_ref, o_ref):
    m = _silu(m_ref[...])
    o_ref[...] = _dot(m.astype(BF16), w_ref[...].astype(BF16)) + b_ref[...]


def _mod_vectors(mvecs, w_mod, b_mod):
    tn = D_MODEL
    return pl.pallas_call(
        _mod_kernel,
        grid=(DEPTH, 3 * D_MODEL // tn),
        in_specs=[
            pl.BlockSpec((SUBLANES, D_MODEL), lambda l, j: (0, 0)),
            pl.BlockSpec((None, D_MODEL, tn), lambda l, j: (l, 0, j)),
            pl.BlockSpec((None, 1, tn), lambda l, j: (l, 0, j)),
        ],
        out_specs=pl.BlockSpec((None, SUBLANES, tn), lambda l, j: (l, 0, j)),
        out_shape=jax.ShapeDtypeStruct((DEPTH, SUBLANES, 3 * D_MODEL), F32),
        compiler_params=pltpu.CompilerParams(vmem_limit_bytes=VMEM_LIMIT),
        name="mod_vectors",
    )(mvecs, w_mod, b_mod.reshape(DEPTH, 1, 3 * D_MODEL))


def _modulated_norm(x, g, mod):
    return _rms(x) * g * (1.0 + mod[1:2, :]) + mod[0:1, :]


def _inproj_kernel(x_ref, mod_ref, g_ref, w_ref, o_ref, h_ref):
    @pl.when(pl.program_id(1) == 0)
    def _():
        h_ref[...] = _modulated_norm(x_ref[...], g_ref[...], mod_ref[...]).astype(BF16)

    o_ref[...] = _dot(h_ref[...], w_ref[...])


def _in_projection(x, mods, mod_row, g, w1, layer, tm):
    t = x.shape[0]
    return pl.pallas_call(
        _inproj_kernel,
        grid=(t // tm, MIX_COLS // TN_IN),
        in_specs=[
            pl.BlockSpec((tm, D_MODEL), lambda i, j: (i, 0)),
            pl.BlockSpec((None, 3, D_MODEL), lambda i, j: (mod_row(i), 0, 0)),
            pl.BlockSpec((1, D_MODEL), lambda i, j: (0, 0)),
            pl.BlockSpec((None, D_MODEL, TN_IN), lambda i, j: (layer, 0, j)),
        ],
        out_specs=pl.BlockSpec((tm, TN_IN), lambda i, j: (i, j)),
        out_shape=jax.ShapeDtypeStruct((t, MIX_COLS), F32),
        scratch_shapes=[pltpu.VMEM((tm, D_MODEL), BF16)],
        compiler_params=pltpu.CompilerParams(
            dimension_semantics=("parallel", "arbitrary"), vmem_limit_bytes=VMEM_LIMIT),
        name="in_projection",
    )(x, mods, g, w1)


def _merge_kernel(x_ref, mod_ref, g_ref, oa_ref, ob_ref, oc_ref, od_ref, wz_ref, wg_ref,
                  wb_ref, wo_ref, *rest, final):
    if final:
        fg_ref, o_ref, y_ref = rest
    else:
        (o_ref,) = rest
    x = x_ref[...]
    mod = mod_ref[...]
    h = _modulated_norm(x, g_ref[...], mod).astype(BF16)
    acc = None
    for k, b_ref in enumerate((oa_ref, ob_ref, oc_ref, od_ref)):
        z = _dot(h, wz_ref[:, k * MIX_W:(k + 1) * MIX_W])
        br = (b_ref[...] * _silu(z)).astype(BF16)
        yb = _dot(br, wb_ref[k])
        term = jax.nn.sigmoid(_dot(h, wg_ref[:, k * D_MODEL:(k + 1) * D_MODEL])) * yb
        acc = term if acc is None else acc + term
    y = _dot(acc.astype(BF16), wo_ref[...])
    out = x + mod[2:3, :] * y
    o_ref[...] = out
    if final:
        y_ref[...] = _rms(out) * fg_ref[...]


def _merge(x, mods, mod_row, g, outs, wz, wg, wb, wo, layer, final_g, tm):
    t = x.shape[0]
    final = final_g is not None
    const = dict(pipeline_mode=pl.Buffered(1))
    in_specs = [
        pl.BlockSpec((tm, D_MODEL), lambda i: (i, 0)),
        pl.BlockSpec((None, 3, D_MODEL), lambda i: (mod_row(i), 0, 0)),
        pl.BlockSpec((1, D_MODEL), lambda i: (0, 0)),
    ] + [pl.BlockSpec((tm, MIX_W), lambda i: (i, 0)) for _ in range(N_BRANCH)] + [
        pl.BlockSpec((None, D_MODEL, N_BRANCH * MIX_W), lambda i: (layer, 0, 0), **const),
        pl.BlockSpec((None, D_MODEL, N_BRANCH * D_MODEL), lambda i: (layer, 0, 0), **const),
        pl.BlockSpec((None, N_BRANCH, MIX_W, D_MODEL), lambda i: (layer, 0, 0, 0), **const),
        pl.BlockSpec((None, D_MODEL, D_MODEL), lambda i: (layer, 0, 0), **const),
    ]
    args = [x, mods, g, *outs, wz, wg, wb, wo]
    out_specs = [pl.BlockSpec((tm, D_MODEL), lambda i: (i, 0))]
    out_shape = [jax.ShapeDtypeStruct((t, D_MODEL), F32)]
    if final:
        in_specs.append(pl.BlockSpec((1, D_MODEL), lambda i: (0, 0)))
        args.append(final_g)
        out_specs.append(pl.BlockSpec((tm, D_MODEL), lambda i: (i, 0)))
        out_shape.append(jax.ShapeDtypeStruct((t, D_MODEL), F32))
    res = pl.pallas_call(
        functools.partial(_merge_kernel, final=final),
        grid=(t // tm,),
        in_specs=in_specs,
        out_specs=out_specs,
        out_shape=out_shape,
        compiler_params=pltpu.CompilerParams(
            dimension_semantics=("parallel",), vmem_limit_bytes=VMEM_LIMIT),
        name="merge",
    )(*args)
    return res if final else (res[0], None)


def _state_specs(layer):
    s_in = pl.BlockSpec((None, None, 2, None, D_HEAD, D_HEAD), lambda b, h: (b, layer, 0, h, 0, 0))
    s_out = pl.BlockSpec((None, 2, None, D_HEAD, D_HEAD), lambda b, h: (b, 0, h, 0, 0))
    return s_in, s_out


def _col_spec(n_tok, cb):
    return pl.BlockSpec((n_tok, LANES), lambda b, h: (b, cb + h))


def _head_mixer_call(kernel, name, bsz, n_tok, in_specs, args, scratch=(), seqs=None):
    _, s_out = _state_specs(0)
    if seqs is not None:
        s_out = pl.BlockSpec((seqs, 2, None, D_HEAD, D_HEAD), lambda b, h: (b, 0, h, 0, 0))
    per_step = seqs or 1
    return pl.pallas_call(
        kernel,
        grid=(bsz // per_step, N_HEAD),
        in_specs=in_specs,
        out_specs=[pl.BlockSpec((per_step * n_tok, LANES), lambda b, h: (b, h)), s_out],
        out_shape=[jax.ShapeDtypeStruct((bsz * n_tok, MIX_W), F32),
                   jax.ShapeDtypeStruct((bsz, 2, N_HEAD, D_HEAD, D_HEAD), F32)],
        scratch_shapes=list(scratch),
        compiler_params=pltpu.CompilerParams(
            dimension_semantics=("parallel", "arbitrary"), vmem_limit_bytes=VMEM_LIMIT),
        name=name,
    )(*args)


def _zero_rows(o_ref, n_tok):
    def body(c, carry):
        o_ref[_rows(c), :] = jnp.zeros((CHUNK, LANES), F32)
        return carry

    lax.fori_loop(0, n_tok // CHUNK, body, 0)


def _norm_rows(o_ref, n_tok, g_ref):
    def body(c, carry):
        o_ref[_rows(c), :] = _rms(o_ref[_rows(c), :]) * g_ref[...]
        return carry

    lax.fori_loop(0, n_tok // CHUNK, body, 0)


def _ret_kernel(lg_ref, dsum_ref, dec_ref, q_ref, k_ref, v_ref, *rest, n_tok, rope, has_init):
    rest = list(rest)
    cos_ref, sin_ref = (rest.pop(0), rest.pop(0)) if rope else (None, None)
    s0_ref = rest.pop(0) if has_init else None
    o_ref, sf_ref, kvb_ref = rest
    h = pl.program_id(1)
    size = RET_CHUNK
    nc = n_tok // size
    dsum = dsum_ref[...]
    dec = dec_ref[...]
    q_dec_f, k_dec_f, q_dec_b, k_dec_b = dec[:, 0:1], dec[:, 1:2], dec[:, 2:3], dec[:, 3:4]
    c_dec_f = jnp.exp(jnp.full((1, D_HEAD), size * lg_ref[0, h], F32))
    c_dec_b = jnp.exp(jnp.full((1, D_HEAD), size * lg_ref[1, h], F32))

    def rotate(x, rows):
        if not rope:
            return x
        return x * cos_ref[rows, :] + pltpu.roll(x, D_HEAD // 2, 1) * sin_ref[rows, :]

    unroll = 2 if nc % 2 == 0 else 1

    def forward(i, s):
        cs = [i * unroll + j for j in range(unroll)]
        rows = [_rows(c, size) for c in cs]
        q = [rotate(q_ref[r, :], r) for r in rows]
        k = [rotate(k_ref[r, :], r) * D_HEAD ** -0.5 for r in rows]
        v = [v_ref[r, :] for r in rows]
        scores = [_dot_nt(qj, kj) for qj, kj in zip(q, k)]
        kv_b = [_dot_tn(kj * k_dec_b, vj) for kj, vj in zip(k, v)]
        kv_f = [_dot_tn(kj * k_dec_f, vj) for kj, vj in zip(k, v)]
        inter = []
        for j in range(unroll):
            inter.append(_dot(q[j] * q_dec_f, s))
            s = s * c_dec_f + kv_f[j]
        for j in range(unroll):
            o_ref[rows[j], :] = _dot(scores[j] * dsum, v[j]) + inter[j]
            kvb_ref[cs[j]] = kv_b[j]
        return s

    s0 = s0_ref[0] if has_init else jnp.zeros((D_HEAD, D_HEAD), F32)
    sf_ref[0] = lax.fori_loop(0, nc // unroll, forward, s0)

    def backward(i, s):
        for j in range(unroll):
            c = nc - 1 - (i * unroll + j)
            rows = _rows(c, size)
            q = rotate(q_ref[rows, :], rows)
            o_ref[rows, :] = _rms(o_ref[rows, :] + _dot(q * q_dec_b, s))
            s = s * c_dec_b + kvb_ref[c]
        return s

    s0 = s0_ref[1] if has_init else jnp.zeros((D_HEAD, D_HEAD), F32)
    sf_ref[1] = lax.fori_loop(0, nc // unroll, backward, s0)


def _retention_tables(log_gamma):
    pos = jnp.arange(RET_CHUNK, dtype=F32)
    rel = pos[:, None] - pos[None, :]
    lg_f, lg_b = log_gamma[0][:, None, None], log_gamma[1][:, None, None]
    lower, upper = rel >= 0, rel <= 0
    dsum = (jnp.where(lower, jnp.exp(jnp.where(lower, rel, 0.0) * lg_f), 0.0)
            + jnp.where(upper, jnp.exp(jnp.where(upper, -rel, 0.0) * lg_b), 0.0))
    lg_f, lg_b = log_gamma[0][:, None], log_gamma[1][:, None]
    dec = jnp.stack([jnp.exp((pos + 1.0) * lg_f), jnp.exp((RET_CHUNK - 1.0 - pos) * lg_f),
                     jnp.exp((RET_CHUNK - pos) * lg_b), jnp.exp(pos * lg_b)], axis=-1)
    return dsum, dec


def _retention(proj, log_gamma, rope, state, layer, bsz, n_tok):
    dsum, dec = _retention_tables(log_gamma)
    in_specs = [pl.BlockSpec(memory_space=pltpu.SMEM),
                pl.BlockSpec((None, RET_CHUNK, RET_CHUNK), lambda b, h: (h, 0, 0)),
                pl.BlockSpec((None, RET_CHUNK, 4), lambda b, h: (h, 0, 0)),
                _col_spec(n_tok, CB_AQ), _col_spec(n_tok, CB_AK), _col_spec(n_tok, CB_AV)]
    args = [log_gamma, dsum, dec, proj, proj, proj]
    if rope is not None:
        in_specs += [pl.BlockSpec((n_tok, LANES), lambda b, h: (0, 0))] * 2
        args += list(rope)
    if state is not None:
        in_specs.append(_state_specs(layer)[0])
        args.append(state)
    kernel = functools.partial(_ret_kernel, n_tok=n_tok, rope=rope is not None,
                               has_init=state is not None)
    scratch = [pltpu.VMEM((n_tok // RET_CHUNK, D_HEAD, D_HEAD), F32)]
    return _head_mixer_call(kernel, "retention", bsz, n_tok, in_specs, args, scratch)


def _tile_scan(a, u, reverse):
    row = lax.broadcasted_iota(jnp.int32, a.shape, 0)
    for s in (1, 2, 4):
        if reverse:
            valid = row < SUBLANES - s
            shift = SUBLANES - s
        else:
            valid = row >= s
            shift = s
        a_sh = jnp.where(valid, pltpu.roll(a, shift, 0), 1.0)
        u_sh = jnp.where(valid, pltpu.roll(u, shift, 0), 0.0)
        u = a * u_sh + u
        a = a * a_sh
    return a, u


def _lru_kernel(x_ref, cw_ref, cb_ref, gw_ref, gb_ref, lam_ref, *rest, n_tok, has_init):
    rest = list(rest)
    h0_ref = rest.pop(0) if has_init else None
    o_ref, hl_ref = rest
    nc = n_tok // CHUNK
    n_tile = CHUNK // SUBLANES
    cw = cw_ref[...]
    cb = cb_ref[...]
    _zero_rows(o_ref, n_tok)

    unroll = min(LRU_UNROLL, nc)
    coef = [-LRU_C * jax.nn.softplus(-lam_ref[d]) for d in range(2)]

    def body(i, carry):
        ctx = [(d, (nc - 1 - (i * unroll + j)) if d == 1 else i * unroll + j)
               for j in range(unroll) for d in range(2)]
        xb = [_conv_chunk(x_ref, cc, nc, n_tok, cw) + cb for _, cc in ctx]
        pre = [(_dot(x, gw_ref[d, 0]), _dot(x, gw_ref[d, 1])) for x, (d, _) in zip(xb, ctx)]
        carry = list(carry)
        for m, (d, cc) in enumerate(ctx):
            r = jax.nn.sigmoid(pre[m][0] + gb_ref[2 * d])
            gate = jax.nn.sigmoid(pre[m][1] + gb_ref[2 * d + 1])
            log_a = coef[d] * r
            a = jnp.exp(log_a)
            one_minus_a2 = -jnp.tanh(log_a) * (a * a + 1.0)
            u = jnp.sqrt(jnp.maximum(one_minus_a2, 1e-12)) * (gate * xb[m])
            scans = [_tile_scan(a[t * SUBLANES:(t + 1) * SUBLANES, :], u[t * SUBLANES:(t + 1) * SUBLANES, :],
                                d == 1) for t in range(n_tile)]
            tiles = [None] * n_tile
            last = 0 if d == 1 else SUBLANES - 1
            for t in (range(n_tile - 1, -1, -1) if d == 1 else range(n_tile)):
                a_t, u_t = scans[t]
                tiles[t] = a_t * carry[d] + u_t
                carry[d] = a_t[last:last + 1, :] * carry[d] + u_t[last:last + 1, :]
            rows = _rows(cc)
            o_ref[rows, :] = o_ref[rows, :] + jnp.concatenate(tiles, axis=0)
        return tuple(carry)

    zero = jnp.zeros((1, LANES), F32)
    h0 = (h0_ref[0], h0_ref[1]) if has_init else (zero, zero)
    h_f, h_b = lax.fori_loop(0, nc // unroll, body, h0)
    hl_ref[0] = h_f
    hl_ref[1] = h_b


def _rglru(proj, conv_w, conv_b, gate_w, gate_b, lam, state, layer, bsz, n_tok):
    n_pair = MIX_W // LANES
    in_specs = [
        pl.BlockSpec((n_tok, LANES), lambda b, p: (b, CB_BX + p)),
        pl.BlockSpec((CONV_K, LANES), lambda b, p: (0, p)),
        pl.BlockSpec((1, LANES), lambda b, p: (0, p)),
        pl.BlockSpec((2, 2, None, LANES, LANES), lambda b, p: (0, 0, p, 0, 0)),
        pl.BlockSpec((4, 1, LANES), lambda b, p: (0, 0, p)),
        pl.BlockSpec((2, 1, LANES), lambda b, p: (0, 0, p)),
    ]
    args = [proj, conv_w, conv_b, gate_w, gate_b, lam]
    if state is not None:
        in_specs.append(pl.BlockSpec((None, None, 2, 1, LANES), lambda b, p: (b, layer, 0, 0, p)))
        args.append(state)
    return pl.pallas_call(
        functools.partial(_lru_kernel, n_tok=n_tok, has_init=state is not None),
        grid=(bsz, n_pair),
        in_specs=in_specs,
        out_specs=[pl.BlockSpec((n_tok, LANES), lambda b, p: (b, p)),
                   pl.BlockSpec((None, 2, 1, LANES), lambda b, p: (b, 0, 0, p))],
        out_shape=[jax.ShapeDtypeStruct((bsz * n_tok, MIX_W), F32),
                   jax.ShapeDtypeStruct((bsz, 2, 1, MIX_W), F32)],
        compiler_params=pltpu.CompilerParams(
            dimension_semantics=("parallel", "arbitrary"), vmem_limit_bytes=VMEM_LIMIT),
        name="rglru",
    )(*args)


def _unit_lower_inverses(mats):
    wide = PACK * CHUNK
    i = lax.broadcasted_iota(jnp.int32, (CHUNK, wide), 0)
    j = lax.broadcasted_iota(jnp.int32, (CHUNK, wide), 1)
    eye = jnp.where(i == j % CHUNK, 1.0, 0.0)
    col_blk = j // CHUNK

    def block_diag(x):
        return jnp.concatenate([jnp.where(col_blk == b, x, jnp.zeros_like(x)) for b in range(PACK)], axis=0)

    steps = CHUNK.bit_length() - 2
    ps = [-jnp.concatenate(mats[g:g + PACK], axis=1) for g in range(0, len(mats), PACK)]
    ts = [eye + p for p in ps]
    halves = [_split2(p) for p in ps]
    diags = [(block_diag(hi), block_diag(lo)) for hi, lo in halves]
    for _ in range(steps):
        ps = [_dot_split(hp, dg) for hp, dg in zip(halves, diags)]
        halves = [_split2(p) for p in ps]
        diags = [(block_diag(hi), block_diag(lo)) for hi, lo in halves]
        ts = [t + _dot_split(_split2(t), dg) for t, dg in zip(ts, diags)]
    return [t[:, b * CHUNK:(b + 1) * CHUNK] for t in ts for b in range(PACK)]


def _gdn_kernel(gp_ref, q_ref, k_ref, v_ref, ab_ref, wq_ref, wk_ref, wv_ref, g_ref,
                *rest, n_tok, seqs, has_init):
    rest = list(rest)
    s0_ref = rest.pop(0) if has_init else None
    o_ref, sf_ref, qt_ref, kw_ref, bm_ref, egl_ref = rest
    h = pl.program_id(1)
    nc = n_tok // CHUNK
    n_rows = seqs * n_tok
    n_chunks = seqs * nc
    group = min(GROUP, n_chunks)
    lane = lax.broadcasted_iota(jnp.int32, (CHUNK, LANES), 1)
    masks = {d: (_order_mask(CHUNK, d == 1, strict=False), _order_mask(CHUNK, d == 1, strict=True))
             for d in range(2)}

    def front(i, carry):
        chunks = [i * group + j for j in range(group)]
        rows = [_rows(c) for c in chunks]
        q, k, v, ab = [], [], [], []
        for c, r in zip(chunks, rows):
            qc = _silu(_conv_chunk(q_ref, c, nc, n_rows, wq_ref[...]))
            kc = _silu(_conv_chunk(k_ref, c, nc, n_rows, wk_ref[...]))
            q.append(qc * (lax.rsqrt(jnp.sum(qc * qc, axis=-1, keepdims=True) + EPS) * D_HEAD ** -0.5))
            k.append(kc * lax.rsqrt(jnp.sum(kc * kc, axis=-1, keepdims=True) + EPS))
            v.append(_silu(_conv_chunk(v_ref, c, nc, n_rows, wv_ref[...])))
            x_ab = ab_ref[r, :]
            ab.append((gp_ref[0:1, :] * jax.nn.softplus(x_ab + gp_ref[1:2, :]), jax.nn.sigmoid(x_ab)))
        ctx = [(j, d) for j in range(group) for d in range(2)]
        beta, both = [], []
        for j, d in ctx:
            incl, strict = masks[d]
            g = jnp.sum(jnp.where(lane == d * N_HEAD + h, ab[j][0], 0.0), axis=1, keepdims=True)
            beta.append(jnp.sum(jnp.where(lane == (2 + d) * N_HEAD + h, ab[j][1], 0.0), axis=1, keepdims=True))
            rhs = jnp.concatenate([jnp.broadcast_to(g, (CHUNK, CHUNK)), g * strict.astype(F32)], axis=1)
            both.append(_dot_mask(incl.astype(BF16), rhs))
        qk_kk = [_dot_nt(jnp.concatenate([q[j], k[j]], axis=0), k[j]) for j in range(group)]
        gc, gam, a_mat = [], [], []
        for m, (j, d) in enumerate(ctx):
            incl, strict = masks[d]
            gc.append(jnp.broadcast_to(both[m][:, 0:1], (CHUNK, LANES)))
            gam.append(jnp.where(incl, jnp.exp(jnp.where(incl, both[m][:, CHUNK:], 0.0)), 0.0))
            a_mat.append(jnp.where(strict, beta[m] * qk_kk[j][CHUNK:] * gam[m], 0.0))
        t_inv = _unit_lower_inverses(a_mat)
        e_gc = [jnp.exp(x) for x in gc]
        wu = [_dot(t_inv[m], jnp.concatenate([k[j] * (beta[m] * e_gc[m]), v[j] * beta[m]], axis=1))
              for m, (j, d) in enumerate(ctx)]
        a_wu = [_dot(qk_kk[j][:CHUNK] * gam[m], wu[m]) for m, (j, d) in enumerate(ctx)]
        gl = [gc[m][(0 if d == 1 else CHUNK - 1):(1 if d == 1 else CHUNK), :] for m, (j, d) in enumerate(ctx)]
        kd_wu = [_dot_tn(k[j] * jnp.exp(gl[m] - gc[m]), wu[m]) for m, (j, d) in enumerate(ctx)]
        for m, (j, d) in enumerate(ctx):
            qt_ref[d, rows[j], :] = (q[j] * e_gc[m] - a_wu[m][:, :D_HEAD]).astype(BF16)
            kw_ref[d, chunks[j]] = kd_wu[m][:, :D_HEAD].astype(BF16)
            bm_ref[d, chunks[j]] = kd_wu[m][:, D_HEAD:]
            egl_ref[d, chunks[j]] = jnp.broadcast_to(jnp.exp(gl[m]), (SUBLANES, LANES))
        for j in range(group):
            o_ref[rows[j], :] = a_wu[2 * j][:, D_HEAD:] + a_wu[2 * j + 1][:, D_HEAD:]
        return carry

    lax.fori_loop(0, n_chunks // group, front, 0)

    unroll = 2 if seqs == 1 else 1
    chains = [(sq, d) for sq in range(seqs) for d in range(2)]

    def back(i, carry):
        s = list(carry)
        for u in range(unroll):
            step = i * unroll + u
            for n, (sq, d) in enumerate(chains):
                c = sq * nc + ((nc - 1 - step) if d == 1 else step)
                rows = _rows(c)
                s_bf = s[n].astype(BF16)
                o_ref[rows, :] = o_ref[rows, :] + _dot(qt_ref[d, rows, :], s_bf)
                s[n] = s[n] * egl_ref[d, c][0:1, :] - _dot(kw_ref[d, c], s_bf) + bm_ref[d, c]
        return tuple(s)

    zero = jnp.zeros((D_HEAD, D_HEAD), F32)
    s0 = tuple(s0_ref[sq, d] if has_init else zero for sq, d in chains)
    s_fin = lax.fori_loop(0, nc // unroll, back, s0)
    for n, (sq, d) in enumerate(chains):
        sf_ref[sq, d] = s_fin[n]
    _norm_rows(o_ref, n_rows, g_ref)


def _gated_delta(proj, neg_a, dt_bias, conv_w, norm_g, state, layer, bsz, n_tok):
    seqs = max(1, min(bsz, GROUP * CHUNK // n_tok))
    rows = seqs * n_tok

    def col_spec(cb):
        return pl.BlockSpec((rows, LANES), lambda b, h: (b, cb + h))

    in_specs = [pl.BlockSpec((2, LANES), lambda b, h: (0, 0)),
                col_spec(CB_CQ), col_spec(CB_CK), col_spec(CB_CV),
                pl.BlockSpec((rows, LANES), lambda b, h: (b, CB_CAB)),
                pl.BlockSpec((CONV_K, LANES), lambda b, h: (0, h)),
                pl.BlockSpec((CONV_K, LANES), lambda b, h: (0, N_HEAD + h)),
                pl.BlockSpec((CONV_K, LANES), lambda b, h: (0, 2 * N_HEAD + h)),
                pl.BlockSpec((1, D_HEAD), lambda b, h: (0, 0))]
    gate_par = jnp.zeros((2, LANES), F32).at[0, :2 * N_HEAD].set(neg_a.reshape(-1)).at[1, :2 * N_HEAD].set(
        dt_bias.reshape(-1))
    args = [gate_par, proj, proj, proj, proj, conv_w, conv_w, conv_w, norm_g]
    if state is not None:
        in_specs.append(pl.BlockSpec((seqs, None, 2, None, D_HEAD, D_HEAD),
                                     lambda b, h: (b, layer, 0, h, 0, 0)))
        args.append(state)
    kernel = functools.partial(_gdn_kernel, n_tok=n_tok, seqs=seqs, has_init=state is not None)
    n_chunks = rows // CHUNK
    scratch = [pltpu.VMEM((2, rows, D_HEAD), BF16),
               pltpu.VMEM((2, n_chunks, D_HEAD, D_HEAD), BF16),
               pltpu.VMEM((2, n_chunks, D_HEAD, D_HEAD), F32),
               pltpu.VMEM((2, n_chunks, SUBLANES, LANES), F32)]
    return _head_mixer_call(kernel, "gated_delta", bsz, n_tok, in_specs, args, scratch, seqs=seqs)


def _block_rows(x, row):
    nb = CHUNK // SUB
    x3 = x.reshape(nb, SUB, LANES)[:, row:row + 1, :]
    return jnp.broadcast_to(x3, (nb, SUB, LANES)).reshape(CHUNK, LANES)


def _hgrn_kernel(q_ref, f0_ref, f1_ref, v_ref, lb_ref, g_ref, *rest, n_tok, seqs, has_init):
    rest = list(rest)
    s0_ref = rest.pop(0) if has_init else None
    o_ref, sf_ref = rest
    nc = n_tok // CHUNK
    nb = CHUNK // SUB
    blk_c = lax.broadcasted_iota(jnp.int32, (CHUNK, 1), 0) // SUB
    blk_i = lax.broadcasted_iota(jnp.int32, (CHUNK, CHUNK), 0) // SUB
    blk_j = lax.broadcasted_iota(jnp.int32, (CHUNK, CHUNK), 1) // SUB
    _zero_rows(o_ref, seqs * n_tok)
    per = max(1, GROUP_D // seqs)

    def body(i, carry):
        ctx = [(2 * sq + d, sq * nc + ((nc - 1 - (i * per + j)) if d == 1 else i * per + j))
               for j in range(per) for sq in range(seqs) for d in range(2)]
        n = len(ctx)
        dirs = [chain % 2 for chain, _ in ctx]
        incl = {d: _order_mask(CHUNK, d == 1, strict=False) for d in range(2)}
        rows = [_rows(cc) for _, cc in ctx]
        q = [_silu(q_ref[r, :]) * D_HEAD ** -0.5 for r in rows]
        v = [v_ref[r, :] for r in rows]
        k, log_f, bc = [], [], []
        for d, r in zip(dirs, rows):
            zf = (f1_ref if d == 1 else f0_ref)[r, :]
            lb = lb_ref[d:d + 1, :]
            k.append((1.0 - lb) * jax.nn.sigmoid(-zf))
            f_gate = lb + (1.0 - lb) * jax.nn.sigmoid(zf)
            log_f.append(jnp.log(jnp.maximum(f_gate, 1e-30)))
            bc.append(_dot_mask(incl[d].astype(BF16), log_f[-1]))
        ex = [b - lf for b, lf in zip(bc, log_f)]
        a_diag = []
        for m, d in enumerate(dirs):
            m_blk = _block_rows(bc[m], SUB // 2 if d == 1 else SUB // 2 - 1)
            a_diag.append(_dot_nt(q[m] * jnp.exp(bc[m] - m_blk), k[m] * jnp.exp(m_blk - bc[m])))
        a_mat = []
        for m, d in enumerate(dirs):
            first = SUB - 1 if d == 1 else 0
            q_c = q[m] * jnp.exp(bc[m] - _block_rows(ex[m], first))
            cross = []
            for blk in range(nb):
                if blk == (nb - 1 if d == 1 else 0):
                    cross.append(jnp.zeros((SUB, CHUNK), F32))
                    continue
                earlier = (blk_c > blk) if d == 1 else (blk_c < blk)
                r_i = ex[m][blk * SUB + first:blk * SUB + first + 1, :]
                k_t = jnp.where(earlier, k[m] * jnp.exp(jnp.where(earlier, r_i - bc[m], 0.0)), 0.0)
                cross.append(_dot_nt(q_c[blk * SUB:(blk + 1) * SUB, :], k_t))
            diag = incl[d] & (blk_i == blk_j)
            a_mat.append(jnp.where(diag, a_diag[m], 0.0) + jnp.concatenate(cross, axis=0))
        b_last = [bc[m][(0 if d == 1 else CHUNK - 1):(1 if d == 1 else CHUNK), :] for m, d in enumerate(dirs)]
        kv = [_dot_tn(v[m], k[m] * jnp.exp(b_last[m] - bc[m])) for m in range(n)]
        intra = [_dot(a_mat[m], v[m]) for m in range(n)]
        st = list(carry)
        for m, (chain, _) in enumerate(ctx):
            o = intra[m] + _dot_nt(q[m] * jnp.exp(bc[m]), st[chain])
            o_ref[rows[m], :] = o_ref[rows[m], :] + o
            st[chain] = st[chain] * jnp.exp(b_last[m]) + kv[m]
        return tuple(st)

    zero = jnp.zeros((D_HEAD, D_HEAD), F32)
    st0 = tuple(s0_ref[sq, d].T if has_init else zero for sq in range(seqs) for d in range(2))
    st_fin = lax.fori_loop(0, nc // per, body, st0)
    for sq in range(seqs):
        for d in range(2):
            sf_ref[sq, d] = st_fin[2 * sq + d].T
    _norm_rows(o_ref, seqs * n_tok, g_ref)


def _hgrn2(proj, lower_bound, norm_g, state, layer, bsz, n_tok):
    seqs = max(1, min(bsz, GROUP_D * CHUNK // n_tok))
    rows = seqs * n_tok
    in_specs = [_col_spec(rows, CB_DQ), _col_spec(rows, CB_DF), _col_spec(rows, CB_DF + N_HEAD),
                _col_spec(rows, CB_DI),
                pl.BlockSpec((2, D_HEAD), lambda b, h: (0, h)),
                pl.BlockSpec((1, D_HEAD), lambda b, h: (0, 0))]
    args = [proj, proj, proj, proj, lower_bound, norm_g]
    if state is not None:
        in_specs.append(pl.BlockSpec((seqs, None, 2, None, D_HEAD, D_HEAD),
                                     lambda b, h: (b, layer, 0, h, 0, 0)))
        args.append(state)
    kernel = functools.partial(_hgrn_kernel, n_tok=n_tok, seqs=seqs, has_init=state is not None)
    return _head_mixer_call(kernel, "hgrn2", bsz, n_tok, in_specs, args, seqs=seqs)


def _rope_tables(n_tok):
    n_freq = D_HEAD // 4
    inv = ROPE_BASE ** (-jnp.arange(n_freq, dtype=F32) / n_freq)
    rows = n_tok // GRID_W
    r = jnp.repeat(jnp.arange(rows, dtype=F32), GRID_W)
    col = jnp.tile(jnp.arange(GRID_W, dtype=F32), rows)
    ang = jnp.concatenate([r[:, None] * inv, col[:, None] * inv], axis=-1)
    cos, sin = jnp.cos(ang), jnp.sin(ang)
    return jnp.concatenate([cos, cos], axis=-1), jnp.concatenate([-sin, sin], axis=-1)


def _split_w_in(w):
    m = MIX_W
    a = w[..., 0:4 * m]
    b = w[..., 4 * m:6 * m]
    c = w[..., 6 * m:10 * m + 16]
    dd = w[..., 10 * m + 16:15 * m + 16]
    mg = w[..., 15 * m + 16:]
    cab = jnp.pad(c[..., 4 * m:], ((0, 0), (0, 0), (0, LANES - 16)))
    w1 = jnp.concatenate([a[..., :3 * m], b[..., :m], c[..., :3 * m], dd[..., :4 * m], cab], axis=-1)
    wz = jnp.concatenate([a[..., 3 * m:], b[..., m:], c[..., 3 * m:4 * m], dd[..., 4 * m:]], axis=-1)
    return w1, wz, mg


def _pair_block_diag(gw):
    z = jnp.zeros_like(gw[..., 0::2, :, :])
    top = jnp.concatenate([gw[..., 0::2, :, :], z], axis=-1)
    bot = jnp.concatenate([z, gw[..., 1::2, :, :]], axis=-1)
    return jnp.concatenate([top, bot], axis=-2)


def kernel(x_prompt, x_sample, state_ret, state_lru, state_gdn, state_hgrn, c, c_ctx, norm_g,
           w_mod, b_mod, w_in, ret_decay, lru_conv_w, lru_conv_b, lru_gate_w, lru_gate_b,
           lru_lambda, gdn_conv_w, gdn_a_log, gdn_dt_bias, gdn_norm_g, hgrn_lb, hgrn_norm_g,
           w_branch, w_out, final_norm_g):
    bp, n_ctx, _ = x_prompt.shape
    bs, n_lat, _ = x_sample.shape

    lb_sm = jax.nn.softmax(hgrn_lb.astype(F32), axis=0)
    lower_bounds = jnp.cumsum(lb_sm, axis=0) - lb_sm[0]
    log_gamma = jnp.log1p(-jnp.exp(ret_decay.astype(F32)))
    gdn_neg_a = -jnp.exp(gdn_a_log.astype(F32))
    rope = _rope_tables(n_lat)
    state_lru5 = state_lru.reshape(bs, DEPTH, 2, 1, MIX_W)
    w1, wz, wg = _split_w_in(w_in.astype(BF16))
    wb, wo = w_branch.astype(BF16), w_out.astype(BF16)
    gate_w = _pair_block_diag(lru_gate_w)

    mvecs = jnp.concatenate([c_ctx[None, :], c, jnp.zeros((SUBLANES - 1 - bs, D_MODEL), F32)], axis=0)
    mods = _mod_vectors(mvecs, w_mod, b_mod).reshape(DEPTH, SUBLANES, 3, D_MODEL)

    tm, tm_in = 512, 1024
    groups = [
        dict(x=x_prompt.reshape(bp * n_ctx, D_MODEL), bsz=bp, n_tok=n_ctx, rope=None, cached=False,
             mod_row=lambda i, t: 0),
        dict(x=x_sample.reshape(bs * n_lat, D_MODEL), bsz=bs, n_tok=n_lat, rope=rope, cached=True,
             mod_row=lambda i, t: 1 + (i * t) // n_lat),
    ]
    new_states = [[], [], [], []]
    finals = []
    for grp in groups:
        x, bsz, n_tok = grp["x"], grp["bsz"], grp["n_tok"]
        cached = grp["cached"]
        for l in range(DEPTH):
            g = norm_g[l][None, :]
            proj = _in_projection(x, mods[l], functools.partial(grp["mod_row"], t=tm_in), g, w1, l, tm_in)
            o_a, s_a = _retention(proj, log_gamma[l], grp["rope"], state_ret if cached else None,
                                  l, bsz, n_tok)
            gate_b = lru_gate_b[l].reshape(4, 1, MIX_W)
            o_b, s_b = _rglru(proj, lru_conv_w[l], lru_conv_b[l][None, :], gate_w[l], gate_b,
                              lru_lambda[l][:, None, :], state_lru5 if cached else None, l, bsz, n_tok)
            o_c, s_c = _gated_delta(proj, gdn_neg_a[l], gdn_dt_bias[l].astype(F32), gdn_conv_w[l],
                                    gdn_norm_g[l][None, :], state_gdn if cached else None, l, bsz, n_tok)
            o_d, s_d = _hgrn2(proj, lower_bounds[l], hgrn_norm_g[l][None, :],
                              state_hgrn if cached else None, l, bsz, n_tok)
            final_g = final_norm_g[None, :] if l == DEPTH - 1 else None
            x, y = _merge(x, mods[l], functools.partial(grp["mod_row"], t=tm), g, (o_a, o_b, o_c, o_d),
                          wz, wg, wb, wo, l, final_g, tm)
            if not cached:
                for acc, s in zip(new_states, (s_a, s_b, s_c, s_d)):
                    acc.append(s)
        finals.append(y)

    y_prompt = finals[0].reshape(bp, n_ctx, D_MODEL)
    y_sample = finals[1].reshape(bs, n_lat, D_MODEL)
    new_ret = jnp.stack(new_states[0], axis=1)
    new_lru = jnp.stack(new_states[1], axis=1).reshape(bp, DEPTH, 2, MIX_W)
    new_gdn = jnp.stack(new_states[2], axis=1)
    new_hgrn = jnp.stack(new_states[3], axis=1)
    return (y_prompt, y_sample, new_ret, new_lru, new_gdn, new_hgrn)
```
